```python
import jax, jax.numpy as jnp
from jax import lax
import numpy as np

D_MODEL = 1024
BATCH = 2
SEQ = 16384
DEPTH = 4
DEC_BATCH = 8
DEC_SEQ = 32
PAST_LEN = 1024

CHUNK = 64
N_EVEN = (DEPTH + 1) // 2
N_ODD = DEPTH // 2
GM_WIDTH = D_MODEL // 2
GM_GROUPS = 4
GM_GROUP_DIM = GM_WIDTH // GM_GROUPS
GM_CHUNK = 128
CV_WIDTH = D_MODEL // 2
CV_KERNEL = 31
CV_STATE = CV_KERNEL - 1
HG_HEADS = 8
HG_DK = D_MODEL // HG_HEADS
HG_DV = D_MODEL // HG_HEADS
HG_WIDTH = HG_HEADS * HG_DK
PK_HEADS = 8
PK_NKEYS = 128
PK_EXPERTS = PK_NKEYS * PK_NKEYS
PK_QDIM = 256
PK_HALF = PK_QDIM // 2
PK_TOPK = 16
PK_BLOCK = 256

EVEN_IN = 2 * GM_WIDTH + 2 * CV_WIDTH
ODD_IN = 4 * HG_WIDTH
EVEN_OUT = GM_WIDTH + CV_WIDTH
EPS = 1e-6

kernel_name = 'hybrid_gmlp_conformer_hgrn2_peer_stream_step'


def _rmsnorm(x, g):
    xf = x.astype(jnp.float32)
    y = xf * lax.rsqrt(jnp.mean(xf * xf, axis=-1, keepdims=True) + EPS)
    return (y * g.astype(jnp.float32)).astype(x.dtype)


def _layernorm(x, g, b):
    xf = x.astype(jnp.float32)
    mu = jnp.mean(xf, axis=-1, keepdims=True)
    xc = xf - mu
    y = xc * lax.rsqrt(jnp.mean(xc * xc, axis=-1, keepdims=True) + EPS)
    return (y * g.astype(jnp.float32) + b.astype(jnp.float32)).astype(x.dtype)


def _spatial_gate(v, ws, bs):
    bn, t, _ = v.shape
    L = min(t, GM_CHUNK)
    mask = jnp.tril(jnp.ones((L, L), dtype=bool))
    w = jnp.where(mask, ws[:, :L, :L], 0)
    vc = v.reshape(bn, t // L, L, GM_GROUPS, GM_GROUP_DIM)
    mixed = jnp.einsum('gts,bnsgc->bntgc', w, vc) + bs[:, :L].T[None, None, :, :, None]
    return mixed.reshape(bn, t, GM_WIDTH)


def _even_mixer(h, conv_prev, w_in, a_g, a_b, ws, bs, cw, cb, b_g, b_b, w_out):
    z = h @ w_in
    za = jax.nn.gelu(z[..., :2 * GM_WIDTH])
    zb = z[..., 2 * GM_WIDTH:]
    u = za[..., :GM_WIDTH]
    v = _layernorm(za[..., GM_WIDTH:], a_g, a_b)
    a_out = u * _spatial_gate(v, ws, bs)
    glu = zb[..., :CV_WIDTH] * jax.nn.sigmoid(zb[..., CV_WIDTH:])
    xpad = jnp.concatenate([conv_prev.astype(glu.dtype), glu], axis=1)
    conv = lax.conv_general_dilated(
        xpad, cw.reshape(CV_KERNEL, 1, CV_WIDTH).astype(xpad.dtype),
        window_strides=(1,), padding='VALID',
        dimension_numbers=('NWC', 'WIO', 'NWC'), feature_group_count=CV_WIDTH) + cb
    b_out = jax.nn.silu(_layernorm(conv, b_g, b_b))
    out = jnp.concatenate([a_out, b_out.astype(a_out.dtype)], axis=-1) @ w_out
    return out, xpad[:, -CV_STATE:], v


def _gla_chunk(S, inp):
    q, k, v, g = inp
    L = q.shape[2]
    b = jnp.cumsum(g, axis=2)
    causal = jnp.tril(jnp.ones((L, L), dtype=bool))
    rel = b[:, :, :, None, :] - b[:, :, None, :, :]
    decay = jnp.exp(jnp.where(causal[:, :, None], rel, -jnp.inf))
    att = jnp.einsum('bhtd,bhtsd,bhsd->bhts', q, decay, k)
    o = att @ v + jnp.einsum('bhtd,bhde->bhte', q * jnp.exp(b), S)
    b_last = b[:, :, -1]
    S_new = jnp.exp(b_last)[..., None] * S + jnp.einsum(
        'bhsd,bhse->bhde', k * jnp.exp(b_last[:, :, None] - b), v)
    return S_new, o


def _hgrn2_mixer(h, S0, lb, w_in, ng, w_out):
    bn, t, _ = h.shape
    z = h @ w_in
    zq, zf, zi, zg = jnp.split(z, 4, axis=-1)
    zf = zf.astype(jnp.float32)
    q = jax.nn.silu(zq.astype(jnp.float32))
    f = lb + (1.0 - lb) * jax.nn.sigmoid(zf)
    k = (1.0 - lb) * jax.nn.sigmoid(-zf)
    logf = jnp.log(f)
    L = min(t, CHUNK)
    nc = t // L

    def chunks(a):
        return a.reshape(bn, nc, L, HG_HEADS, -1).transpose(1, 0, 3, 2, 4)

    S_fin, o = lax.scan(_gla_chunk, S0.astype(jnp.float32),
                        (chunks(q), chunks(k), chunks(zi.astype(jnp.float32)), chunks(logf)))
    o = o.transpose(1, 0, 3, 2, 4).reshape(bn, t, HG_HEADS, HG_DV)
    o = _rmsnorm(o, ng) * jax.nn.silu(zg.astype(jnp.float32)).reshape(bn, t, HG_HEADS, HG_DV)
    return o.reshape(bn, t, HG_WIDTH) @ w_out, S_fin


def _peer(h, wq, keys, tab_u, tab_v):
    bn, t, d = h.shape
    n = bn * t
    blk = min(PK_BLOCK, n)
    pad = (-n) % blk
    flat = jnp.pad(h.reshape(n, d), ((0, pad), (0, 0)))

    def block(hb):
        tb = hb.shape[0]
        q = (hb @ wq).reshape(tb, PK_HEADS, 2, PK_HALF)
        s = jnp.einsum('thpc,hpkc->thpk', q, keys).astype(jnp.float32)
        s_top, i_top = lax.top_k(s, PK_TOPK)
        cand = (s_top[:, :, 0, :, None] + s_top[:, :, 1, None, :]).reshape(tb, PK_HEADS, PK_TOPK * PK_TOPK)
        cidx = (i_top[:, :, 0, :, None] * PK_NKEYS + i_top[:, :, 1, None, :]).reshape(tb, PK_HEADS, PK_TOPK * PK_TOPK)
        best, pos = lax.top_k(cand, PK_TOPK)
        eidx = jnp.take_along_axis(cidx, pos, axis=-1)
        gate = jax.nn.softmax(best, axis=-1)
        u = jnp.take(tab_u, eidx, axis=0)
        act = jax.nn.gelu(jnp.einsum('thkd,td->thk', u, hb).astype(jnp.float32))
        v = jnp.take(tab_v, eidx, axis=0)
        return jnp.einsum('thk,thkd->td', (gate * act).astype(v.dtype), v)

    out = lax.map(block, flat.reshape(-1, blk, d))
    return out.reshape(-1, d)[:n].reshape(bn, t, d)


def setup_inputs(seed: int = 0) -> dict:
    key = jax.random.key(seed)
    ks = jax.random.split(key, 32)

    def nrm(k, shape, scale):
        return jax.random.normal(k, shape, jnp.float32) * scale

    return {
        'x_prompt': nrm(ks[0], (BATCH, SEQ, D_MODEL), 1.0),
        'x_sample': nrm(ks[1], (DEC_BATCH, DEC_SEQ, D_MODEL), 1.0),
        'state_conv': nrm(ks[2], (N_EVEN, DEC_BATCH, CV_STATE, CV_WIDTH), 0.5),
        'state_hgrn': nrm(ks[3], (N_ODD, DEC_BATCH, HG_HEADS, HG_DK, HG_DV), 0.3),
        'norm_mix': 1.0 + nrm(ks[4], (DEPTH, D_MODEL), 0.02),
        'norm_ffn': 1.0 + nrm(ks[5], (DEPTH, D_MODEL), 0.02),
        'norm_final': 1.0 + nrm(ks[6], (D_MODEL,), 0.02),
        'ev_w_in': nrm(ks[7], (N_EVEN, D_MODEL, EVEN_IN), D_MODEL ** -0.5),
        'ev_a_ln_g': 1.0 + nrm(ks[8], (N_EVEN, GM_WIDTH), 0.02),
        'ev_a_ln_b': nrm(ks[9], (N_EVEN, GM_WIDTH), 0.02),
        'ev_ws': nrm(ks[10], (N_EVEN, GM_GROUPS, GM_CHUNK, GM_CHUNK), 0.5 * GM_CHUNK ** -0.5),
        'ev_bs': 1.0 + nrm(ks[11], (N_EVEN, GM_GROUPS, GM_CHUNK), 0.1),
        'ev_conv_w': nrm(ks[12], (N_EVEN, CV_KERNEL, CV_WIDTH), CV_KERNEL ** -0.5),
        'ev_conv_b': nrm(ks[13], (N_EVEN, CV_WIDTH), 0.02),
        'ev_b_ln_g': 1.0 + nrm(ks[14], (N_EVEN, CV_WIDTH), 0.02),
        'ev_b_ln_b': nrm(ks[15], (N_EVEN, CV_WIDTH), 0.02),
        'ev_w_out': nrm(ks[16], (N_EVEN, EVEN_OUT, D_MODEL), EVEN_OUT ** -0.5),
        'od_w_in': nrm(ks[17], (N_ODD, D_MODEL, ODD_IN), D_MODEL ** -0.5),
        'od_lower': nrm(ks[18], (DEPTH, HG_WIDTH), 0.1),
        'od_norm_g': 1.0 + nrm(ks[19], (N_ODD, HG_DV), 0.02),
        'od_w_out': nrm(ks[20], (N_ODD, HG_WIDTH, D_MODEL), HG_WIDTH ** -0.5),
        'peer_wq': nrm(ks[21], (DEPTH, D_MODEL, PK_HEADS * PK_QDIM), D_MODEL ** -0.5),
        'peer_keys': nrm(ks[22], (DEPTH, PK_HEADS, 2, PK_NKEYS, PK_HALF), PK_HALF ** -0.5),
        'peer_u': nrm(ks[23], (DEPTH, PK_EXPERTS, D_MODEL), D_MODEL ** -0.5),
        'peer_v': nrm(ks[24], (DEPTH, PK_EXPERTS, D_MODEL), D_MODEL ** -0.5),
    }


def reference(x_prompt, x_sample, state_conv, state_hgrn, norm_mix, norm_ffn, norm_final,
              ev_w_in, ev_a_ln_g, ev_a_ln_b, ev_ws, ev_bs, ev_conv_w, ev_conv_b, ev_b_ln_g, ev_b_ln_b, ev_w_out,
              od_w_in, od_lower, od_norm_g, od_w_out, peer_wq, peer_keys, peer_u, peer_v):
    lb_soft = jax.nn.softmax(od_lower.astype(jnp.float32), axis=0)
    lb_all = jnp.cumsum(lb_soft, axis=0) - lb_soft[0]
    xp, xs = x_prompt, x_sample
    conv_p, conv_s, hg_p, hg_s, v_s = [], [], [], [], []
    for l in range(DEPTH):
        j = l // 2
        hp = _rmsnorm(xp, norm_mix[l])
        hs = _rmsnorm(xs, norm_mix[l])
        if l % 2 == 0:
            prm = (ev_w_in[j], ev_a_ln_g[j], ev_a_ln_b[j], ev_ws[j], ev_bs[j], ev_conv_w[j], ev_conv_b[j],
                   ev_b_ln_g[j], ev_b_ln_b[j], ev_w_out[j])
            zeros = jnp.zeros((xp.shape[0], CV_STATE, CV_WIDTH), xp.dtype)
            mp, cp, _ = _even_mixer(hp, zeros, *prm)
            ms, cs, vs = _even_mixer(hs, state_conv[j], *prm)
            conv_p.append(cp)
            conv_s.append(cs)
            v_s.append(vs)
        else:
            S0 = jnp.zeros((xp.shape[0], HG_HEADS, HG_DK, HG_DV), jnp.float32)
            mp, Sp = _hgrn2_mixer(hp, S0, lb_all[l], od_w_in[j], od_norm_g[j], od_w_out[j])
            ms, Ss = _hgrn2_mixer(hs, state_hgrn[j], lb_all[l], od_w_in[j], od_norm_g[j], od_w_out[j])
            hg_p.append(Sp)
            hg_s.append(Ss)
        xp = xp + mp.astype(xp.dtype)
        xs = xs + ms.astype(xs.dtype)
        xp = xp + _peer(_rmsnorm(xp, norm_ffn[l]), peer_wq[l], peer_keys[l], peer_u[l], peer_v[l]).astype(xp.dtype)
        xs = xs + _peer(_rmsnorm(xs, norm_ffn[l]), peer_wq[l], peer_keys[l], peer_u[l], peer_v[l]).astype(xs.dtype)
    y_prompt = _rmsnorm(xp, norm_final)
    y_sample = _rmsnorm(xs, norm_final)
    return (y_prompt, y_sample, jnp.stack(conv_p), jnp.stack(conv_s), jnp.stack(hg_p), jnp.stack(hg_s), jnp.stack(v_s))
```

```python
import functools

import jax
import jax.numpy as jnp
from jax import lax
from jax.experimental import pallas as pl
from jax.experimental.pallas import tpu as pltpu

F32 = jnp.float32
BF16 = jnp.bfloat16
HIGHEST = lax.Precision.HIGHEST

SUB = 8
LANE = 128

D_MODEL = 1024
BATCH = 2
SEQ = 16384
DEPTH = 4
DEC_BATCH = 8
DEC_SEQ = 32
N_PROMPT = BATCH * SEQ
N_SAMPLE = DEC_BATCH * DEC_SEQ
N_TOK = N_PROMPT + N_SAMPLE
EPS = 1e-6

GM_WIDTH = D_MODEL // 2
GM_GROUPS = 4
GM_GROUP_DIM = GM_WIDTH // GM_GROUPS
GM_CHUNK = 128
CV_WIDTH = D_MODEL // 2
CV_KERNEL = 31
CV_STATE = CV_KERNEL - 1
CV_PAD = 32
HG_HEADS = 8
HG_DK = D_MODEL // HG_HEADS
HG_CHUNK = 64
PK_HEADS = 8
PK_NKEYS = 128
PK_EXPERTS = PK_NKEYS * PK_NKEYS
PK_HALF = 128
PK_TOPK = 16
PK_PAIRS = PK_HEADS * PK_TOPK
HALF = D_MODEL // 2
HSUB = HALF // LANE

MIX_TB = 256
SEL_TB = 256
PEER_TB = 128
VMEM_LIMIT = 48 * 1024 * 1024


def _gelu_tanh(x):
    return 0.5 * x * (1.0 + jnp.tanh(0.7978845608028654 * (x + 0.044715 * x * x * x)))


def _silu(x):
    return x * jax.nn.sigmoid(x)


def _rms(x, g):
    return x * lax.rsqrt(jnp.mean(x * x, axis=-1, keepdims=True) + EPS) * g


def _ln(x, g, b):
    xc = x - jnp.mean(x, axis=-1, keepdims=True)
    return xc * lax.rsqrt(jnp.mean(xc * xc, axis=-1, keepdims=True) + EPS) * g + b


def _bdot(a, b):
    return jnp.dot(a.astype(BF16), b.astype(BF16), preferred_element_type=F32)


def _even_kernel(emit_v, x_ref, prev_ref, nm_ref, win_ref, ag_ref, ab_ref, ws_ref, bst_ref, cw_ref, cb_ref,
                 bg_ref, bb_ref, wout_ref, o_ref, cst_ref, *rest):
    if emit_v:
        v_ref, pbuf = rest
    else:
        (pbuf,) = rest
    tb = x_ref.shape[0]
    gl = min(tb, GM_CHUNK)

    @pl.when(pl.program_id(1) == 0)
    def _():
        pbuf[0:CV_PAD, :] = prev_ref[0]

    x = x_ref[...]
    z = _bdot(_rms(x, nm_ref[...]), win_ref[...])
    za = _gelu_tanh(z[:, :2 * GM_WIDTH])
    u = za[:, :GM_WIDTH]
    v = _ln(za[:, GM_WIDTH:], ag_ref[...], ab_ref[...])
    if emit_v:
        v_ref[0] = v

    row = lax.broadcasted_iota(jnp.int32, (gl, gl), 0)
    col = lax.broadcasted_iota(jnp.int32, (gl, gl), 1)
    wts = [jnp.where(row >= col, ws_ref[g, 0:gl, 0:gl], 0.0) for g in range(GM_GROUPS)]
    chunks = []
    for c in range(tb // gl):
        vc = v[c * gl:(c + 1) * gl]
        parts = [jnp.dot(wts[g], vc[:, g * GM_GROUP_DIM:(g + 1) * GM_GROUP_DIM], precision=HIGHEST,
                         preferred_element_type=F32) + bst_ref[0:gl, g:g + 1] for g in range(GM_GROUPS)]
        chunks.append(jnp.concatenate(parts, axis=1))
    a_out = u * jnp.concatenate(chunks, axis=0)

    zb = z[:, 2 * GM_WIDTH:]
    pbuf[CV_PAD:CV_PAD + tb, :] = zb[:, :CV_WIDTH] * jax.nn.sigmoid(zb[:, CV_WIDTH:])
    first = CV_PAD - CV_STATE
    conv = cb_ref[...] + cw_ref[0:1, :] * pbuf[first:first + tb, :]
    for j in range(1, CV_KERNEL):
        conv = conv + cw_ref[j:j + 1, :] * pbuf[first + j:first + j + tb, :]
    b_out = _silu(_ln(conv, bg_ref[...], bb_ref[...]))

    o_ref[...] = x + _bdot(jnp.concatenate([a_out, b_out], axis=1), wout_ref[...])
    tail = pbuf[tb:tb + CV_PAD, :]
    cst_ref[0] = tail
    pbuf[0:CV_PAD, :] = tail


def _even_call(xf, prev, nm, win, ag, ab, ws, bst, cw, cb, bg, bb, wout, *, nb, t_len, tb, blk0, emit_v):
    nt = t_len // tb
    whole = lambda a: pl.BlockSpec(a.shape, lambda b, t: (0,) * a.ndim)
    x_spec = pl.BlockSpec((tb, D_MODEL), lambda b, t: (blk0 + b * nt + t, 0))
    params = (nm, win, ag, ab, ws, bst, cw, cb, bg, bb, wout)
    out_shape = [jax.ShapeDtypeStruct(xf.shape, F32), jax.ShapeDtypeStruct((nb, CV_PAD, CV_WIDTH), F32)]
    out_specs = [x_spec, pl.BlockSpec((1, CV_PAD, CV_WIDTH), lambda b, t: (b, 0, 0))]
    if emit_v:
        out_shape.append(jax.ShapeDtypeStruct((nb, t_len, GM_WIDTH), F32))
        out_specs.append(pl.BlockSpec((1, tb, GM_WIDTH), lambda b, t: (b, t, 0)))
    return pl.pallas_call(
        functools.partial(_even_kernel, emit_v),
        grid=(nb, nt),
        in_specs=[x_spec, pl.BlockSpec((1, CV_PAD, CV_WIDTH), lambda b, t: (b, 0, 0))] + [whole(a) for a in params],
        out_specs=out_specs,
        out_shape=out_shape,
        scratch_shapes=[pltpu.VMEM((CV_PAD + tb, CV_WIDTH), F32)],
        input_output_aliases={0: 0},
        compiler_params=pltpu.CompilerParams(
            dimension_semantics=("arbitrary", "arbitrary"), vmem_limit_bytes=VMEM_LIMIT),
        name="even_mixer",
    )(xf, prev, *params)


def _odd_kernel(layer, x_ref, s0_ref, nm_ref, lower_ref, win_ref, ng_ref, wout_ref, o_ref, sout_ref,
                st_s, q_s, k_s, v_s, b_s, o_s):
    tb = x_ref.shape[0]
    cl = min(tb, HG_CHUNK)
    nck = tb // cl
    t = pl.program_id(1)

    @pl.when(t == 0)
    def _():
        for hh in range(HG_HEADS):
            st_s[hh] = s0_ref[0, hh].T

    x = x_ref[...]
    z = _bdot(_rms(x, nm_ref[...]), win_ref[...])
    low = lower_ref[...]
    ex = jnp.exp(low - jnp.max(low, axis=0, keepdims=True))
    soft = ex / jnp.sum(ex, axis=0, keepdims=True)
    lb = jnp.sum(soft[1:layer + 1], axis=0, keepdims=True)
    zf = z[:, D_MODEL:2 * D_MODEL]
    q = _silu(z[:, :D_MODEL])
    logf = jnp.log(lb + (1.0 - lb) * jax.nn.sigmoid(zf))
    k = (1.0 - lb) * jax.nn.sigmoid(-zf)
    vv = z[:, 2 * D_MODEL:3 * D_MODEL]

    row = lax.broadcasted_iota(jnp.int32, (cl, cl), 0)
    col = lax.broadcasted_iota(jnp.int32, (cl, cl), 1)
    tril = (row >= col).astype(F32)
    for c in range(nck):
        bc = jnp.dot(tril, logf[c * cl:(c + 1) * cl], precision=HIGHEST, preferred_element_type=F32)
        for hh in range(HG_HEADS):
            b_s[hh, c * cl:(c + 1) * cl, :] = bc[:, hh * HG_DK:(hh + 1) * HG_DK]
    for hh in range(HG_HEADS):
        sl = slice(hh * HG_DK, (hh + 1) * HG_DK)
        q_s[hh] = q[:, sl]
        k_s[hh] = k[:, sl]
        v_s[hh] = vv[:, sl]

    nblk = cl // SUB
    rowi = lax.broadcasted_iota(jnp.int32, (SUB, HG_DK), 0)

    def chunk_head(i, carry):
        c = i // HG_HEADS
        hh = i % HG_HEADS
        rows = pl.ds(pl.multiple_of(c * cl, cl), cl)
        qh = q_s[hh, rows, :]
        kh = k_s[hh, rows, :]
        vh = v_s[hh, rows, :]
        bh = b_s[hh, rows, :]
        st = st_s[hh]
        o = lax.dot_general((qh * jnp.exp(bh)).astype(BF16), st.astype(BF16), (((1,), (1,)), ((), ())),
                            preferred_element_type=F32)
        qb = [qh[SUB * r:SUB * (r + 1)] for r in range(nblk)]
        bb = [bh[SUB * r:SUB * (r + 1)] for r in range(nblk)]
        ob = [o[SUB * r:SUB * (r + 1)] for r in range(nblk)]
        for s in range(cl):
            b_src = bh[s:s + 1]
            k_src = kh[s:s + 1]
            v_src = vh[s:s + 1]
            for r in range(s // SUB, nblk):
                rel = bb[r] - b_src
                if r == s // SUB:
                    rel = jnp.where(rowi >= (s % SUB), rel, -jnp.inf)
                att = jnp.sum(qb[r] * jnp.exp(rel) * k_src, axis=-1, keepdims=True)
                ob[r] = ob[r] + att * v_src
        o_s[hh, rows, :] = jnp.concatenate(ob, axis=0)
        b_last = bh[cl - 1:cl]
        kt = kh * jnp.exp(b_last - bh)
        upd = lax.dot_general(vh.astype(BF16), kt.astype(BF16), (((0,), (0,)), ((), ())),
                              preferred_element_type=F32)
        st_s[hh] = st * jnp.exp(b_last) + upd
        return carry

    lax.fori_loop(0, nck * HG_HEADS, chunk_head, 0)

    parts = []
    for hh in range(HG_HEADS):
        oh = o_s[hh]
        on = oh * lax.rsqrt(jnp.mean(oh * oh, axis=-1, keepdims=True) + EPS) * ng_ref[...]
        parts.append(on * _silu(z[:, 3 * D_MODEL + hh * HG_DK:3 * D_MODEL + (hh + 1) * HG_DK]))
    o_ref[...] = x + _bdot(jnp.concatenate(parts, axis=1), wout_ref[...])

    @pl.when(t == pl.num_programs(1) - 1)
    def _():
        for hh in range(HG_HEADS):
            sout_ref[0, hh] = st_s[hh].T


def _odd_call(xf, s0, nm, lower, win, ng, wout, *, layer, nb, t_len, tb, blk0):
    nt = t_len // tb
    whole = lambda a: pl.BlockSpec(a.shape, lambda b, t: (0,) * a.ndim)
    x_spec = pl.BlockSpec((tb, D_MODEL), lambda b, t: (blk0 + b * nt + t, 0))
    s_spec = pl.BlockSpec((1, HG_HEADS, HG_DK, HG_DK), lambda b, t: (b, 0, 0, 0))
    params = (nm, lower, win, ng, wout)
    head_buf = pltpu.VMEM((HG_HEADS, tb, HG_DK), F32)
    return pl.pallas_call(
        functools.partial(_odd_kernel, layer),
        grid=(nb, nt),
        in_specs=[x_spec, s_spec] + [whole(a) for a in params],
        out_specs=[x_spec, s_spec],
        out_shape=[jax.ShapeDtypeStruct(xf.shape, F32),
                   jax.ShapeDtypeStruct((nb, HG_HEADS, HG_DK, HG_DK), F32)],
        scratch_shapes=[pltpu.VMEM((HG_HEADS, HG_DK, HG_DK), F32), head_buf, head_buf, head_buf, head_buf, head_buf],
        input_output_aliases={0: 0},
        compiler_params=pltpu.CompilerParams(
            dimension_semantics=("arbitrary", "arbitrary"), vmem_limit_bytes=VMEM_LIMIT),
        name="odd_mixer",
    )(xf, s0, *params)


def _peer_select_kernel(x_ref, nf_ref, wq_ref, keys_ref, eidx_ref, gate_ref, st_s, tv_s, ti_s, es_s, gs_s):
    tb = x_ref.shape[0]
    nlt = tb // LANE
    q = _bdot(_rms(x_ref[...], nf_ref[...]), wq_ref[...])
    for hp in range(2 * PK_HEADS):
        sc = lax.dot_general(keys_ref[hp], q[:, hp * PK_HALF:(hp + 1) * PK_HALF].astype(BF16),
                             (((1,), (1,)), ((), ())), preferred_element_type=F32)
        for lt in range(nlt):
            st_s[hp, lt] = sc[:, lt * LANE:(lt + 1) * LANE]

    kiota = lax.broadcasted_iota(jnp.int32, (PK_NKEYS, LANE), 0)
    nrow = [PK_TOPK if a == 0 else SUB for a in range(PK_TOPK)]
    pos = jnp.concatenate([a * PK_TOPK + lax.broadcasted_iota(jnp.int32, (nrow[a], LANE), 0)
                           for a in range(PK_TOPK)], axis=0)
    big = PK_TOPK * PK_TOPK

    for lt in range(nlt):
        def first(hp, carry):
            s = st_s[hp, lt]
            for r in range(PK_TOPK):
                m = jnp.max(s, axis=0, keepdims=True)
                ix = jnp.min(jnp.where(s == m, kiota, PK_NKEYS), axis=0, keepdims=True)
                tv_s[hp, lt, r:r + 1, :] = m
                ti_s[hp, lt, r:r + 1, :] = ix
                s = jnp.where(kiota == ix, -jnp.inf, s)
            return carry

        lax.fori_loop(0, 2 * PK_HEADS, first, 0)

        def second(hd, carry):
            s1 = tv_s[2 * hd, lt]
            i1 = ti_s[2 * hd, lt]
            s2 = tv_s[2 * hd + 1, lt]
            i2 = ti_s[2 * hd + 1, lt]
            cand = jnp.concatenate([s1[a:a + 1] + s2[0:nrow[a]] for a in range(PK_TOPK)], axis=0)
            cidx = jnp.concatenate([i1[a:a + 1] * PK_NKEYS + i2[0:nrow[a]] for a in range(PK_TOPK)], axis=0)
            best = []
            for r in range(PK_TOPK):
                m = jnp.max(cand, axis=0, keepdims=True)
                pick = jnp.min(jnp.where(cand == m, pos, big), axis=0, keepdims=True)
                sel = pos == pick
                es_s[lt, hd, r:r + 1, :] = jnp.max(jnp.where(sel, cidx, -1), axis=0, keepdims=True).astype(F32)
                best.append(m)
                cand = jnp.where(sel, -jnp.inf, cand)
            ex = [jnp.exp(b - best[0]) for b in best]
            den = ex[0]
            for r in range(1, PK_TOPK):
                den = den + ex[r]
            for r in range(PK_TOPK):
                gs_s[lt, hd, r:r + 1, :] = ex[r] / den
            return carry

        lax.fori_loop(0, PK_HEADS, second, 0)
        rows = slice(lt * LANE, (lt + 1) * LANE)
        eidx_ref[rows, :] = es_s[lt].reshape(PK_PAIRS, LANE).T.astype(jnp.int32)
        gate_ref[rows, :] = gs_s[lt].reshape(PK_PAIRS, LANE).T


def _peer_select(xf, nf, wq, keys):
    n_tok = xf.shape[0]
    nlt = SEL_TB // LANE
    whole = lambda a: pl.BlockSpec(a.shape, lambda i: (0,) * a.ndim)
    return pl.pallas_call(
        _peer_select_kernel,
        grid=(n_tok // SEL_TB,),
        in_specs=[pl.BlockSpec((SEL_TB, D_MODEL), lambda i: (i, 0)), whole(nf), whole(wq), whole(keys)],
        out_specs=[pl.BlockSpec((SEL_TB, PK_PAIRS), lambda i: (i, 0)),
                   pl.BlockSpec((SEL_TB, PK_PAIRS), lambda i: (i, 0))],
        out_shape=[jax.ShapeDtypeStruct((n_tok, PK_PAIRS), jnp.int32),
                   jax.ShapeDtypeStruct((n_tok, PK_PAIRS), F32)],
        scratch_shapes=[pltpu.VMEM((2 * PK_HEADS, nlt, PK_NKEYS, LANE), F32),
                        pltpu.VMEM((2 * PK_HEADS, nlt, PK_TOPK, LANE), F32),
                        pltpu.VMEM((2 * PK_HEADS, nlt, PK_TOPK, LANE), jnp.int32),
                        pltpu.VMEM((nlt, PK_HEADS, PK_TOPK, LANE), F32),
                        pltpu.VMEM((nlt, PK_HEADS, PK_TOPK, LANE), F32)],
        compiler_params=pltpu.CompilerParams(
            dimension_semantics=("arbitrary",), vmem_limit_bytes=VMEM_LIMIT),
        name="peer_select",
    )(xf, nf, wq, keys)


def _pack_table(tab):
    b = lax.bitcast_convert_type(tab.astype(BF16), jnp.uint16).astype(jnp.uint32)
    return (b[:, :HALF] | (b[:, HALF:] << 16)).reshape(PK_EXPERTS, HSUB, LANE)


def _unpack_row(word):
    lo = lax.bitcast_convert_type(word << jnp.uint32(16), F32)
    hi = lax.bitcast_convert_type(word & jnp.uint32(0xFFFF0000), F32)
    return lo, hi


def _peer_u_kernel(idx_hbm, x_ref, nf_ref, gate_ref, tab_ref, w_ref, idx_smem, sem, hlo3, hhi3, rows3, acc_ref):
    n = PEER_TB * PK_PAIRS
    cp = pltpu.make_async_copy(idx_hbm.at[pl.ds(pl.program_id(0) * n, n)], idx_smem, sem)
    cp.start()
    h = _rms(x_ref[...], nf_ref[...])
    for s in range(HSUB):
        hlo3[:, s, :] = h[:, s * LANE:(s + 1) * LANE]
        hhi3[:, s, :] = h[:, HALF + s * LANE:HALF + (s + 1) * LANE]
    cp.wait()
    lane = lax.broadcasted_iota(jnp.int32, (PK_PAIRS, PEER_TB), 1)

    def tok(t, carry):
        hlo = hlo3[t]
        hhi = hhi3[t]
        for k in range(PK_PAIRS):
            lo, hi = _unpack_row(tab_ref[idx_smem[t * PK_PAIRS + k]])
            rows3[k] = lo * hlo + hi * hhi
        part = rows3[:, 0, :]
        for s in range(1, HSUB):
            part = part + rows3[:, s, :]
        red = jnp.sum(part, axis=-1, keepdims=True)
        acc_ref[...] = jnp.where(lane == t, red, acc_ref[...])
        return carry

    lax.fori_loop(0, PEER_TB, tok, 0)
    w_ref[...] = gate_ref[...] * _gelu_tanh(acc_ref[...].T)


def _peer_u(idx_flat, xf, nf, gate, tab_packed):
    n_tok = xf.shape[0]
    return pl.pallas_call(
        _peer_u_kernel,
        grid=(n_tok // PEER_TB,),
        in_specs=[
            pl.BlockSpec(memory_space=pl.ANY),
            pl.BlockSpec((PEER_TB, D_MODEL), lambda i: (i, 0)),
            pl.BlockSpec(nf.shape, lambda i: (0, 0)),
            pl.BlockSpec((PEER_TB, PK_PAIRS), lambda i: (i, 0)),
            pl.BlockSpec(memory_space=pltpu.VMEM),
        ],
        out_specs=pl.BlockSpec((PEER_TB, PK_PAIRS), lambda i: (i, 0)),
        out_shape=jax.ShapeDtypeStruct((n_tok, PK_PAIRS), F32),
        scratch_shapes=[
            pltpu.SMEM((PEER_TB * PK_PAIRS,), jnp.int32),
            pltpu.SemaphoreType.DMA,
            pltpu.VMEM((PEER_TB, HSUB, LANE), F32),
            pltpu.VMEM((PEER_TB, HSUB, LANE), F32),
            pltpu.VMEM((PK_PAIRS, HSUB, LANE), F32),
            pltpu.VMEM((PK_PAIRS, PEER_TB), F32),
        ],
        compiler_params=pltpu.CompilerParams(
            dimension_semantics=("arbitrary",), vmem_limit_bytes=VMEM_LIMIT),
        name="peer_u",
    )(idx_flat, xf, nf, gate, tab_packed)


def _peer_v_kernel(idx_hbm, w_hbm, x_ref, tab_ref, o_ref, idx_smem, w_smem, sem, rlo3, rhi3):
    n = PEER_TB * PK_PAIRS
    cp_i = pltpu.make_async_copy(idx_hbm.at[pl.ds(pl.program_id(0) * n, n)], idx_smem, sem.at[0])
    cp_w = pltpu.make_async_copy(w_hbm.at[pl.ds(pl.program_id(0) * n, n)], w_smem, sem.at[1])
    cp_i.start()
    cp_w.start()
    cp_i.wait()
    cp_w.wait()

    def tok(t, carry):
        alo = [jnp.zeros((HSUB, LANE), F32) for _ in range(2)]
        ahi = [jnp.zeros((HSUB, LANE), F32) for _ in range(2)]
        for k in range(PK_PAIRS):
            wt = w_smem[t * PK_PAIRS + k]
            lo, hi = _unpack_row(tab_ref[idx_smem[t * PK_PAIRS + k]])
            alo[k % 2] = alo[k % 2] + wt * lo
            ahi[k % 2] = ahi[k % 2] + wt * hi
        rlo3[t] = alo[0] + alo[1]
        rhi3[t] = ahi[0] + ahi[1]
        return carry

    lax.fori_loop(0, PEER_TB, tok, 0)
    for s in range(HSUB):
        lo_cols = slice(s * LANE, (s + 1) * LANE)
        hi_cols = slice(HALF + s * LANE, HALF + (s + 1) * LANE)
        o_ref[:, lo_cols] = x_ref[:, lo_cols] + rlo3[:, s, :]
        o_ref[:, hi_cols] = x_ref[:, hi_cols] + rhi3[:, s, :]


def _peer_v(idx_flat, w_flat, xf, tab_packed):
    n_tok = xf.shape[0]
    return pl.pallas_call(
        _peer_v_kernel,
        grid=(n_tok // PEER_TB,),
        in_specs=[
            pl.BlockSpec(memory_space=pl.ANY),
            pl.BlockSpec(memory_space=pl.ANY),
            pl.BlockSpec((PEER_TB, D_MODEL), lambda i: (i, 0)),
            pl.BlockSpec(memory_space=pltpu.VMEM),
        ],
        out_specs=pl.BlockSpec((PEER_TB, D_MODEL), lambda i: (i, 0)),
        out_shape=jax.ShapeDtypeStruct((n_tok, D_MODEL), F32),
        scratch_shapes=[
            pltpu.SMEM((PEER_TB * PK_PAIRS,), jnp.int32),
            pltpu.SMEM((PEER_TB * PK_PAIRS,), F32),
            pltpu.SemaphoreType.DMA((2,)),
            pltpu.VMEM((PEER_TB, HSUB, LANE), F32),
            pltpu.VMEM((PEER_TB, HSUB, LANE), F32),
        ],
        compiler_params=pltpu.CompilerParams(
            dimension_semantics=("arbitrary",), vmem_limit_bytes=VMEM_LIMIT),
        name="peer_v",
    )(idx_flat, w_flat, xf, tab_packed)


def _final_kernel(x_ref, g_ref, o_ref):
    o_ref[...] = _rms(x_ref[...], g_ref[...])


def _final_call(xf, g, *, n_rows, blk0):
    nblk = n_rows // MIX_TB
    return pl.pallas_call(
        _final_kernel,
        grid=(nblk,),
        in_specs=[pl.BlockSpec((MIX_TB, D_MODEL), lambda i: (blk0 + i, 0)), pl.BlockSpec(g.shape, lambda i: (0, 0))],
        out_specs=pl.BlockSpec((MIX_TB, D_MODEL), lambda i: (i, 0)),
        out_shape=jax.ShapeDtypeStruct((n_rows, D_MODEL), F32),
        compiler_params=pltpu.CompilerParams(dimension_semantics=("arbitrary",)),
        name="final_norm",
    )(xf, g)


def kernel(x_prompt, x_sample, state_conv, state_hgrn, norm_mix, norm_ffn, norm_final, ev_w_in, ev_a_ln_g, ev_a_ln_b, ev_ws, ev_bs, ev_conv_w, ev_conv_b, ev_b_ln_g, ev_b_ln_b, ev_w_out, od_w_in, od_lower, od_norm_g, od_w_out, peer_wq, peer_keys, peer_u, peer_v):
    row = lambda a: a.reshape(1, -1)
    xf = jnp.concatenate([x_prompt.reshape(N_PROMPT, D_MODEL), x_sample.reshape(N_SAMPLE, D_MODEL)], axis=0)
    pad_hist = lambda a: jnp.pad(a, ((0, 0), (CV_PAD - CV_STATE, 0), (0, 0)))
    conv_p, conv_s, hg_p, hg_s, v_s = [], [], [], [], []
    prompt = dict(nb=BATCH, t_len=SEQ, tb=MIX_TB, blk0=0)
    sample = dict(nb=DEC_BATCH, t_len=DEC_SEQ, tb=DEC_SEQ, blk0=N_PROMPT // DEC_SEQ)
    for l in range(DEPTH):
        j = l // 2
        if l % 2 == 0:
            prm = (row(norm_mix[l]), ev_w_in[j].astype(BF16), row(ev_a_ln_g[j]), row(ev_a_ln_b[j]), ev_ws[j],
                   ev_bs[j].T, ev_conv_w[j], row(ev_conv_b[j]), row(ev_b_ln_g[j]), row(ev_b_ln_b[j]),
                   ev_w_out[j].astype(BF16))
            zeros = jnp.zeros((BATCH, CV_PAD, CV_WIDTH), F32)
            xf, cp = _even_call(xf, zeros, *prm, emit_v=False, **prompt)
            xf, cs, vs = _even_call(xf, pad_hist(state_conv[j]), *prm, emit_v=True, **sample)
            conv_p.append(cp[:, CV_PAD - CV_STATE:])
            conv_s.append(cs[:, CV_PAD - CV_STATE:])
            v_s.append(vs)
        else:
            prm = (row(norm_mix[l]), od_lower, od_w_in[j].astype(BF16), row(od_norm_g[j]), od_w_out[j].astype(BF16))
            zeros = jnp.zeros((BATCH, HG_HEADS, HG_DK, HG_DK), F32)
            xf, sp = _odd_call(xf, zeros, *prm, layer=l, **prompt)
            xf, ss = _odd_call(xf, state_hgrn[j], *prm, layer=l, **sample)
            hg_p.append(sp)
            hg_s.append(ss)
        nf = row(norm_ffn[l])
        keys = peer_keys[l].reshape(2 * PK_HEADS, PK_NKEYS, PK_HALF).astype(BF16)
        eidx, gate = _peer_select(xf, nf, peer_wq[l].astype(BF16), keys)
        idx_flat = eidx.reshape(-1)
        w = _peer_u(idx_flat, xf, nf, gate, _pack_table(peer_u[l]))
        xf = _peer_v(idx_flat, w.reshape(-1), xf, _pack_table(peer_v[l]))
    g = row(norm_final)
    y_prompt = _final_call(xf, g, n_rows=N_PROMPT, blk0=0).reshape(BATCH, SEQ, D_MODEL)
    y_sample = _final_call(xf, g, n_rows=N_SAMPLE, blk0=N_PROMPT // MIX_TB).reshape(DEC_BATCH, DEC_SEQ, D_MODEL)
    return (y_prompt, y_sample, jnp.stack(conv_p), jnp.stack(conv_s), jnp.stack(hg_p), jnp.stack(hg_s),
            jnp.stack(v_s))
```

```python
import functools

import jax
import jax.numpy as jnp
from jax import lax
from jax.experimental import pallas as pl
from jax.experimental.pallas import tpu as pltpu

F32 = jnp.float32
BF16 = jnp.bfloat16
HIGHEST = lax.Precision.HIGHEST

SUB = 8
LANE = 128

D_MODEL = 1024
BATCH = 2
SEQ = 16384
DEPTH = 4
DEC_BATCH = 8
DEC_SEQ = 32
N_PROMPT = BATCH * SEQ
N_SAMPLE = DEC_BATCH * DEC_SEQ
N_TOK = N_PROMPT + N_SAMPLE
EPS = 1e-6

GM_WIDTH = D_MODEL // 2
GM_GROUPS = 4
GM_GROUP_DIM = GM_WIDTH // GM_GROUPS
GM_CHUNK = 128
CV_WIDTH = D_MODEL // 2
CV_KERNEL = 31
CV_STATE = CV_KERNEL - 1
CV_PAD = 32
HG_HEADS = 8
HG_DK = D_MODEL // HG_HEADS
HG_CHUNK = 64
PK_HEADS = 8
PK_NKEYS = 128
PK_EXPERTS = PK_NKEYS * PK_NKEYS
PK_HALF = 128
PK_TOPK = 16
PK_PAIRS = PK_HEADS * PK_TOPK
HALF = D_MODEL // 2
HSUB = HALF // LANE
STAGE_STRIDE = 136

MIX_TB = 256
SEL_TB = 256
PEER_TB = 128
VMEM_LIMIT = 48 * 1024 * 1024


def _gelu_tanh(x):
    return 0.5 * x * (1.0 + jnp.tanh(0.7978845608028654 * (x + 0.044715 * x * x * x)))


def _silu(x):
    return x * jax.nn.sigmoid(x)


def _rms(x, g):
    return x * lax.rsqrt(jnp.mean(x * x, axis=-1, keepdims=True) + EPS) * g


def _ln(x, g, b):
    xc = x - jnp.mean(x, axis=-1, keepdims=True)
    return xc * lax.rsqrt(jnp.mean(xc * xc, axis=-1, keepdims=True) + EPS) * g + b


def _bdot(a, b):
    return jnp.dot(a.astype(BF16), b.astype(BF16), preferred_element_type=F32)


def _even_kernel(emit_v, x_ref, prev_ref, nm_ref, win_ref, ag_ref, ab_ref, ws_ref, bst_ref, cw_ref, cb_ref,
                 bg_ref, bb_ref, wout_ref, o_ref, cst_ref, *rest):
    if emit_v:
        v_ref, pbuf = rest
    else:
        (pbuf,) = rest
    tb = x_ref.shape[0]
    gl = min(tb, GM_CHUNK)

    @pl.when(pl.program_id(1) == 0)
    def _():
        pbuf[0:CV_PAD, :] = prev_ref[0]

    x = x_ref[...]
    z = _bdot(_rms(x, nm_ref[...]), win_ref[...])
    za = _gelu_tanh(z[:, :2 * GM_WIDTH])
    u = za[:, :GM_WIDTH]
    v = _ln(za[:, GM_WIDTH:], ag_ref[...], ab_ref[...])
    if emit_v:
        v_ref[0] = v

    row = lax.broadcasted_iota(jnp.int32, (gl, gl), 0)
    col = lax.broadcasted_iota(jnp.int32, (gl, gl), 1)
    wts = [jnp.where(row >= col, ws_ref[g, 0:gl, 0:gl], 0.0) for g in range(GM_GROUPS)]
    chunks = []
    for c in range(tb // gl):
        vc = v[c * gl:(c + 1) * gl]
        parts = [jnp.dot(wts[g], vc[:, g * GM_GROUP_DIM:(g + 1) * GM_GROUP_DIM], precision=HIGHEST,
                         preferred_element_type=F32) + bst_ref[0:gl, g:g + 1] for g in range(GM_GROUPS)]
        chunks.append(jnp.concatenate(parts, axis=1))
    a_out = u * jnp.concatenate(chunks, axis=0)

    zb = z[:, 2 * GM_WIDTH:]
    pbuf[CV_PAD:CV_PAD + tb, :] = zb[:, :CV_WIDTH] * jax.nn.sigmoid(zb[:, CV_WIDTH:])
    first = CV_PAD - CV_STATE
    conv = cb_ref[...] + cw_ref[0:1, :] * pbuf[first:first + tb, :]
    for j in range(1, CV_KERNEL):
        conv = conv + cw_ref[j:j + 1, :] * pbuf[first + j:first + j + tb, :]
    b_out = _silu(_ln(conv, bg_ref[...], bb_ref[...]))

    o_ref[...] = x + _bdot(jnp.concatenate([a_out, b_out], axis=1), wout_ref[...])
    tail = pbuf[tb:tb + CV_PAD, :]
    cst_ref[0] = tail
    pbuf[0:CV_PAD, :] = tail


def _even_call(xf, prev, nm, win, ag, ab, ws, bst, cw, cb, bg, bb, wout, *, nb, t_len, tb, blk0, emit_v):
    nt = t_len // tb
    whole = lambda a: pl.BlockSpec(a.shape, lambda b, t: (0,) * a.ndim)
    x_spec = pl.BlockSpec((tb, D_MODEL), lambda b, t: (blk0 + b * nt + t, 0))
    params = (nm, win, ag, ab, ws, bst, cw, cb, bg, bb, wout)
    out_shape = [jax.ShapeDtypeStruct(xf.shape, F32), jax.ShapeDtypeStruct((nb, CV_PAD, CV_WIDTH), F32)]
    out_specs = [x_spec, pl.BlockSpec((1, CV_PAD, CV_WIDTH), lambda b, t: (b, 0, 0))]
    if emit_v:
        out_shape.append(jax.ShapeDtypeStruct((nb, t_len, GM_WIDTH), F32))
        out_specs.append(pl.BlockSpec((1, tb, GM_WIDTH), lambda b, t: (b, t, 0)))
    return pl.pallas_call(
        functools.partial(_even_kernel, emit_v),
        grid=(nb, nt),
        in_specs=[x_spec, pl.BlockSpec((1, CV_PAD, CV_WIDTH), lambda b, t: (b, 0, 0))] + [whole(a) for a in params],
        out_specs=out_specs,
        out_shape=out_shape,
        scratch_shapes=[pltpu.VMEM((CV_PAD + tb, CV_WIDTH), F32)],
        input_output_aliases={0: 0},
        compiler_params=pltpu.CompilerParams(
            dimension_semantics=("arbitrary", "arbitrary"), vmem_limit_bytes=VMEM_LIMIT),
        name="even_mixer",
    )(xf, prev, *params)


def _odd_kernel(layer, x_ref, s0_ref, nm_ref, lower_ref, win_ref, ng_ref, wout_ref, o_ref, sout_ref,
                st_s, q_s, k_s, v_s, b_s, o_s):
    tb = x_ref.shape[0]
    cl = min(tb, HG_CHUNK)
    nck = tb // cl
    t = pl.program_id(1)

    @pl.when(t == 0)
    def _():
        for hh in range(HG_HEADS):
            st_s[hh] = s0_ref[0, hh].T

    x = x_ref[...]
    z = _bdot(_rms(x, nm_ref[...]), win_ref[...])
    low = lower_ref[...]
    ex = jnp.exp(low - jnp.max(low, axis=0, keepdims=True))
    soft = ex / jnp.sum(ex, axis=0, keepdims=True)
    lb = jnp.sum(soft[1:layer + 1], axis=0, keepdims=True)
    zf = z[:, D_MODEL:2 * D_MODEL]
    q = _silu(z[:, :D_MODEL])
    logf = jnp.log(lb + (1.0 - lb) * jax.nn.sigmoid(zf))
    k = (1.0 - lb) * jax.nn.sigmoid(-zf)
    vv = z[:, 2 * D_MODEL:3 * D_MODEL]

    row = lax.broadcasted_iota(jnp.int32, (cl, cl), 0)
    col = lax.broadcasted_iota(jnp.int32, (cl, cl), 1)
    tril = (row >= col).astype(F32)
    for c in range(nck):
        bc = jnp.dot(tril, logf[c * cl:(c + 1) * cl], precision=HIGHEST, preferred_element_type=F32)
        for hh in range(HG_HEADS):
            b_s[hh, c * cl:(c + 1) * cl, :] = bc[:, hh * HG_DK:(hh + 1) * HG_DK]
    for hh in range(HG_HEADS):
        sl = slice(hh * HG_DK, (hh + 1) * HG_DK)
        q_s[hh] = q[:, sl]
        k_s[hh] = k[:, sl]
        v_s[hh] = vv[:, sl]

    nblk = cl // SUB
    rowi = lax.broadcasted_iota(jnp.int32, (SUB, HG_DK), 0)

    def chunk_head(i, carry):
        c = i // HG_HEADS
        hh = i % HG_HEADS
        rows = pl.ds(pl.multiple_of(c * cl, cl), cl)
        qh = q_s[hh, rows, :]
        kh = k_s[hh, rows, :]
        vh = v_s[hh, rows, :]
        bh = b_s[hh, rows, :]
        st = st_s[hh]
        o = lax.dot_general((qh * jnp.exp(bh)).astype(BF16), st.astype(BF16), (((1,), (1,)), ((), ())),
                            preferred_element_type=F32)
        qb = [qh[SUB * r:SUB * (r + 1)] for r in range(nblk)]
        bb = [bh[SUB * r:SUB * (r + 1)] for r in range(nblk)]
        ob = [o[SUB * r:SUB * (r + 1)] for r in range(nblk)]
        for s in range(cl):
            b_src = bh[s:s + 1]
            k_src = kh[s:s + 1]
            v_src = vh[s:s + 1]
            for r in range(s // SUB, nblk):
                rel = bb[r] - b_src
                if r == s // SUB:
                    rel = jnp.where(rowi >= (s % SUB), rel, -jnp.inf)
                att = jnp.sum(qb[r] * jnp.exp(rel) * k_src, axis=-1, keepdims=True)
                ob[r] = ob[r] + att * v_src
        o_s[hh, rows, :] = jnp.concatenate(ob, axis=0)
        b_last = bh[cl - 1:cl]
        kt = kh * jnp.exp(b_last - bh)
        upd = lax.dot_general(vh.astype(BF16), kt.astype(BF16), (((0,), (0,)), ((), ())),
                              preferred_element_type=F32)
        st_s[hh] = st * jnp.exp(b_last) + upd
        return carry

    lax.fori_loop(0, nck * HG_HEADS, chunk_head, 0)

    parts = []
    for hh in range(HG_HEADS):
        oh = o_s[hh]
        on = oh * lax.rsqrt(jnp.mean(oh * oh, axis=-1, keepdims=True) + EPS) * ng_ref[...]
        parts.append(on * _silu(z[:, 3 * D_MODEL + hh * HG_DK:3 * D_MODEL + (hh + 1) * HG_DK]))
    o_ref[...] = x + _bdot(jnp.concatenate(parts, axis=1), wout_ref[...])

    @pl.when(t == pl.num_programs(1) - 1)
    def _():
        for hh in range(HG_HEADS):
            sout_ref[0, hh] = st_s[hh].T


def _odd_call(xf, s0, nm, lower, win, ng, wout, *, layer, nb, t_len, tb, blk0):
    nt = t_len // tb
    whole = lambda a: pl.BlockSpec(a.shape, lambda b, t: (0,) * a.ndim)
    x_spec = pl.BlockSpec((tb, D_MODEL), lambda b, t: (blk0 + b * nt + t, 0))
    s_spec = pl.BlockSpec((1, HG_HEADS, HG_DK, HG_DK), lambda b, t: (b, 0, 0, 0))
    params = (nm, lower, win, ng, wout)
    head_buf = pltpu.VMEM((HG_HEADS, tb, HG_DK), F32)
    return pl.pallas_call(
        functools.partial(_odd_kernel, layer),
        grid=(nb, nt),
        in_specs=[x_spec, s_spec] + [whole(a) for a in params],
        out_specs=[x_spec, s_spec],
        out_shape=[jax.ShapeDtypeStruct(xf.shape, F32),
                   jax.ShapeDtypeStruct((nb, HG_HEADS, HG_DK, HG_DK), F32)],
        scratch_shapes=[pltpu.VMEM((HG_HEADS, HG_DK, HG_DK), F32), head_buf, head_buf, head_buf, head_buf, head_buf],
        input_output_aliases={0: 0},
        compiler_params=pltpu.CompilerParams(
            dimension_semantics=("arbitrary", "arbitrary"), vmem_limit_bytes=VMEM_LIMIT),
        name="odd_mixer",
    )(xf, s0, *params)


def _peer_select_kernel(x_ref, nf_ref, wq_ref, keys_ref, eidx_ref, gate_ref, st_s, tv_s, ti_s, es_s, gs_s):
    tb = x_ref.shape[0]
    nlt = tb // LANE
    q = _bdot(_rms(x_ref[...], nf_ref[...]), wq_ref[...])
    for hp in range(2 * PK_HEADS):
        sc = lax.dot_general(keys_ref[hp], q[:, hp * PK_HALF:(hp + 1) * PK_HALF].astype(BF16),
                             (((1,), (1,)), ((), ())), preferred_element_type=F32)
        for lt in range(nlt):
            st_s[hp, lt] = sc[:, lt * LANE:(lt + 1) * LANE]

    kiota = lax.broadcasted_iota(jnp.int32, (PK_NKEYS, LANE), 0)
    nrow = [PK_TOPK if a == 0 else SUB for a in range(PK_TOPK)]
    pos = jnp.concatenate([a * PK_TOPK + lax.broadcasted_iota(jnp.int32, (nrow[a], LANE), 0)
                           for a in range(PK_TOPK)], axis=0)
    big = PK_TOPK * PK_TOPK

    streams1 = [(p, lt) for lt in range(nlt) for p in range(2)]

    def head(hd, carry):
        ss = [st_s[2 * hd + p, lt] for p, lt in streams1]
        for r in range(PK_TOPK):
            for n, (p, lt) in enumerate(streams1):
                s = ss[n]
                m = jnp.max(s, axis=0, keepdims=True)
                ix = jnp.min(jnp.where(s == m, kiota, PK_NKEYS), axis=0, keepdims=True)
                tv_s[2 * hd + p, lt, r:r + 1, :] = m
                ti_s[2 * hd + p, lt, r:r + 1, :] = ix
                ss[n] = jnp.where(kiota == ix, -jnp.inf, s)
        cands, cidxs, bests = [], [], [[] for _ in range(nlt)]
        for lt in range(nlt):
            s1 = tv_s[2 * hd, lt]
            i1 = ti_s[2 * hd, lt]
            s2 = tv_s[2 * hd + 1, lt]
            i2 = ti_s[2 * hd + 1, lt]
            cands.append(jnp.concatenate([s1[a:a + 1] + s2[0:nrow[a]] for a in range(PK_TOPK)], axis=0))
            cidxs.append(jnp.concatenate([i1[a:a + 1] * PK_NKEYS + i2[0:nrow[a]] for a in range(PK_TOPK)], axis=0))
        for r in range(PK_TOPK):
            for lt in range(nlt):
                cand = cands[lt]
                m = jnp.max(cand, axis=0, keepdims=True)
                pick = jnp.min(jnp.where(cand == m, pos, big), axis=0, keepdims=True)
                sel = pos == pick
                es_s[lt, hd, r:r + 1, :] = jnp.max(jnp.where(sel, cidxs[lt], -1), axis=0,
                                                   keepdims=True).astype(F32)
                bests[lt].append(m)
                cands[lt] = jnp.where(sel, -jnp.inf, cand)
        for lt in range(nlt):
            ex = [jnp.exp(b - bests[lt][0]) for b in bests[lt]]
            den = ex[0]
            for r in range(1, PK_TOPK):
                den = den + ex[r]
            for r in range(PK_TOPK):
                gs_s[lt, hd, r:r + 1, :] = ex[r] / den
        return carry

    lax.fori_loop(0, PK_HEADS, head, 0)
    for lt in range(nlt):
        rows = slice(lt * LANE, (lt + 1) * LANE)
        eidx_ref[rows, :] = es_s[lt].reshape(PK_PAIRS, LANE).T.astype(jnp.int32)
        gate_ref[rows, :] = gs_s[lt].reshape(PK_PAIRS, LANE).T


def _peer_select(xf, nf, wq, keys):
    n_tok = xf.shape[0]
    nlt = SEL_TB // LANE
    whole = lambda a: pl.BlockSpec(a.shape, lambda i: (0,) * a.ndim)
    return pl.pallas_call(
        _peer_select_kernel,
        grid=(n_tok // SEL_TB,),
        in_specs=[pl.BlockSpec((SEL_TB, D_MODEL), lambda i: (i, 0)), whole(nf), whole(wq), whole(keys)],
        out_specs=[pl.BlockSpec((SEL_TB, PK_PAIRS), lambda i: (i, 0)),
                   pl.BlockSpec((SEL_TB, PK_PAIRS), lambda i: (i, 0))],
        out_shape=[jax.ShapeDtypeStruct((n_tok, PK_PAIRS), jnp.int32),
                   jax.ShapeDtypeStruct((n_tok, PK_PAIRS), F32)],
        scratch_shapes=[pltpu.VMEM((2 * PK_HEADS, nlt, PK_NKEYS, LANE), F32),
                        pltpu.VMEM((2 * PK_HEADS, nlt, PK_TOPK, LANE), F32),
                        pltpu.VMEM((2 * PK_HEADS, nlt, PK_TOPK, LANE), jnp.int32),
                        pltpu.VMEM((nlt, PK_HEADS, PK_TOPK, LANE), F32),
                        pltpu.VMEM((nlt, PK_HEADS, PK_TOPK, LANE), F32)],
        compiler_params=pltpu.CompilerParams(
            dimension_semantics=("arbitrary",), vmem_limit_bytes=VMEM_LIMIT),
        name="peer_select",
    )(xf, nf, wq, keys)


def _pack_table(tab):
    b = lax.bitcast_convert_type(tab.astype(BF16), jnp.uint16).astype(jnp.uint32)
    return (b[:, :HALF] | (b[:, HALF:] << 16)).reshape(PK_EXPERTS, HSUB, LANE)


N_PIECE = 2


def _split(x):
    p1 = x.astype(BF16).astype(F32)
    p2 = (x - p1).astype(BF16).astype(F32)
    return p1, p2


def _gather_rows(tab_ref, idx_smem, t, stage):
    for k in range(PK_PAIRS):
        stage[pl.ds(k, HSUB, stride=STAGE_STRIDE), :] = tab_ref[idx_smem[t, k]]


def _stage_block(stage, s):
    return pltpu.bitcast(stage[s * STAGE_STRIDE:s * STAGE_STRIDE + PK_PAIRS, :], BF16)


def _token_pipeline(tab_ref, idx_smem, stage, compute):
    _gather_rows(tab_ref, idx_smem, 0, stage.at[0])

    def body(t, carry):
        cur = t % 2
        compute(t, stage.at[cur])
        _gather_rows(tab_ref, idx_smem, jnp.minimum(t + 1, PEER_TB - 1), stage.at[1 - cur])
        return carry

    lax.fori_loop(0, PEER_TB, body, 0)


def _peer_u_kernel(idx_hbm, x_ref, nf_ref, gate_ref, tab_ref, w_ref, idx_smem, sem, hs, stage, ybuf):
    cp = pltpu.make_async_copy(idx_hbm.at[pl.ds(pl.program_id(0) * PEER_TB, PEER_TB)], idx_smem, sem)
    cp.start()
    pieces = _split(_rms(x_ref[...], nf_ref[...]))
    hs[:, :, 2 * N_PIECE:, :] = jnp.zeros((PEER_TB, HSUB, SUB - 2 * N_PIECE, LANE), F32)
    for i in range(N_PIECE):
        for s in range(HSUB):
            hs[:, s, i, :] = pieces[i][:, s * LANE:(s + 1) * LANE]
            hs[:, s, N_PIECE + i, :] = pieces[i][:, HALF + s * LANE:HALF + (s + 1) * LANE]
    cp.wait()
    rowi = lax.broadcasted_iota(jnp.int32, (SUB, 2 * PK_PAIRS), 0)
    odd = lax.broadcasted_iota(jnp.int32, (SUB, 2 * PK_PAIRS), 1) % 2
    keep = jnp.where(odd == 0, (rowi < N_PIECE).astype(F32),
                     ((rowi >= N_PIECE) & (rowi < 2 * N_PIECE)).astype(F32))

    def compute(t, stage):
        acc = jnp.zeros((SUB, 2 * PK_PAIRS), F32)
        for s in range(HSUB):
            acc = acc + lax.dot_general(hs[t, s].astype(BF16), _stage_block(stage, s), (((1,), (1,)), ((), ())),
                                        preferred_element_type=F32)
        ybuf[pl.ds(t, 1), :] = jnp.sum(acc * keep, axis=0, keepdims=True)

    _token_pipeline(tab_ref, idx_smem, stage, compute)
    r = lax.broadcasted_iota(jnp.int32, (2 * PK_PAIRS, PK_PAIRS), 0)
    c = lax.broadcasted_iota(jnp.int32, (2 * PK_PAIRS, PK_PAIRS), 1)
    fold = (r // 2 == c).astype(F32)
    act = jnp.dot(ybuf[...], fold, precision=HIGHEST, preferred_element_type=F32)
    w_ref[...] = gate_ref[...] * _gelu_tanh(act)


def _peer_u(eidx, xf, nf, gate, tab_packed):
    n_tok = xf.shape[0]
    stage = pltpu.VMEM((2, HSUB * STAGE_STRIDE, LANE), jnp.uint32)
    return pl.pallas_call(
        _peer_u_kernel,
        grid=(n_tok // PEER_TB,),
        in_specs=[
            pl.BlockSpec(memory_space=pl.ANY),
            pl.BlockSpec((PEER_TB, D_MODEL), lambda i: (i, 0)),
            pl.BlockSpec(nf.shape, lambda i: (0, 0)),
            pl.BlockSpec((PEER_TB, PK_PAIRS), lambda i: (i, 0)),
            pl.BlockSpec(memory_space=pltpu.VMEM),
        ],
        out_specs=pl.BlockSpec((PEER_TB, PK_PAIRS), lambda i: (i, 0)),
        out_shape=jax.ShapeDtypeStruct((n_tok, PK_PAIRS), F32),
        scratch_shapes=[
            pltpu.SMEM((PEER_TB, PK_PAIRS), jnp.int32),
            pltpu.SemaphoreType.DMA,
            pltpu.VMEM((PEER_TB, HSUB, SUB, LANE), F32),
            stage,
            pltpu.VMEM((PEER_TB, 2 * PK_PAIRS), F32),
        ],
        compiler_params=pltpu.CompilerParams(
            dimension_semantics=("arbitrary",), vmem_limit_bytes=VMEM_LIMIT),
        name="peer_u",
    )(eidx, xf, nf, gate, tab_packed)


def _peer_v_kernel(idx_hbm, w_ref, x_ref, tab_ref, o_ref, idx_smem, sem, av, stage, ov):
    cp = pltpu.make_async_copy(idx_hbm.at[pl.ds(pl.program_id(0) * PEER_TB, PEER_TB)], idx_smem, sem)
    cp.start()
    r = lax.broadcasted_iota(jnp.int32, (PK_PAIRS, 2 * PK_PAIRS), 0)
    c = lax.broadcasted_iota(jnp.int32, (PK_PAIRS, 2 * PK_PAIRS), 1)
    spread = [(c == 2 * r).astype(BF16), (c == 2 * r + 1).astype(BF16)]
    pieces = _split(w_ref[...])
    av[:, 2 * N_PIECE:, :] = jnp.zeros((PEER_TB, SUB - 2 * N_PIECE, 2 * PK_PAIRS), F32)
    for i in range(N_PIECE):
        for j in range(2):
            av[:, N_PIECE * j + i, :] = jnp.dot(pieces[i].astype(BF16), spread[j], preferred_element_type=F32)
    cp.wait()

    def compute(t, stage):
        wt = av[t].astype(BF16)
        lo, hi = [], []
        for s in range(HSUB):
            y = jnp.dot(wt, _stage_block(stage, s), preferred_element_type=F32)
            lo.append(y[0:1] + y[1:2])
            hi.append(y[2:3] + y[3:4])
        ov[pl.ds(t, 1), :] = jnp.concatenate(lo + hi, axis=1)

    _token_pipeline(tab_ref, idx_smem, stage, compute)
    o_ref[...] = x_ref[...] + ov[...]


def _peer_v(eidx, w, xf, tab_packed):
    n_tok = xf.shape[0]
    stage = pltpu.VMEM((2, HSUB * STAGE_STRIDE, LANE), jnp.uint32)
    return pl.pallas_call(
        _peer_v_kernel,
        grid=(n_tok // PEER_TB,),
        in_specs=[
            pl.BlockSpec(memory_space=pl.ANY),
            pl.BlockSpec((PEER_TB, PK_PAIRS), lambda i: (i, 0)),
            pl.BlockSpec((PEER_TB, D_MODEL), lambda i: (i, 0)),
            pl.BlockSpec(memory_space=pltpu.VMEM),
        ],
        out_specs=pl.BlockSpec((PEER_TB, D_MODEL), lambda i: (i, 0)),
        out_shape=jax.ShapeDtypeStruct((n_tok, D_MODEL), F32),
        scratch_shapes=[
            pltpu.SMEM((PEER_TB, PK_PAIRS), jnp.int32),
            pltpu.SemaphoreType.DMA,
            pltpu.VMEM((PEER_TB, SUB, 2 * PK_PAIRS), F32),
            stage,
            pltpu.VMEM((PEER_TB, D_MODEL), F32),
        ],
        compiler_params=pltpu.CompilerParams(
            dimension_semantics=("arbitrary",), vmem_limit_bytes=VMEM_LIMIT),
        name="peer_v",
    )(eidx, w, xf, tab_packed)


def _final_kernel(x_ref, g_ref, o_ref):
    o_ref[...] = _rms(x_ref[...], g_ref[...])


def _final_call(xf, g, *, n_rows, blk0):
    nblk = n_rows // MIX_TB
    return pl.pallas_call(
        _final_kernel,
        grid=(nblk,),
        in_specs=[pl.BlockSpec((MIX_TB, D_MODEL), lambda i: (blk0 + i, 0)), pl.BlockSpec(g.shape, lambda i: (0, 0))],
        out_specs=pl.BlockSpec((MIX_TB, D_MODEL), lambda i: (i, 0)),
        out_shape=jax.ShapeDtypeStruct((n_rows, D_MODEL), F32),
        compiler_params=pltpu.CompilerParams(dimension_semantics=("arbitrary",)),
        name="final_norm",
    )(xf, g)


def kernel(x_prompt, x_sample, state_conv, state_hgrn, norm_mix, norm_ffn, norm_final, ev_w_in, ev_a_ln_g, ev_a_ln_b, ev_ws, ev_bs, ev_conv_w, ev_conv_b, ev_b_ln_g, ev_b_ln_b, ev_w_out, od_w_in, od_lower, od_norm_g, od_w_out, peer_wq, peer_keys, peer_u, peer_v):
    row = lambda a: a.reshape(1, -1)
    xf = jnp.concatenate([x_prompt.reshape(N_PROMPT, D_MODEL), x_sample.reshape(N_SAMPLE, D_MODEL)], axis=0)
    pad_hist = lambda a: jnp.pad(a, ((0, 0), (CV_PAD - CV_STATE, 0), (0, 0)))
    conv_p, conv_s, hg_p, hg_s, v_s = [], [], [], [], []
    prompt = dict(nb=BATCH, t_len=SEQ, tb=MIX_TB, blk0=0)
    sample = dict(nb=DEC_BATCH, t_len=DEC_SEQ, tb=DEC_SEQ, blk0=N_PROMPT // DEC_SEQ)
    for l in range(DEPTH):
        j = l // 2
        if l % 2 == 0:
            prm = (row(norm_mix[l]), ev_w_in[j].astype(BF16), row(ev_a_ln_g[j]), row(ev_a_ln_b[j]), ev_ws[j],
                   ev_bs[j].T, ev_conv_w[j], row(ev_conv_b[j]), row(ev_b_ln_g[j]), row(ev_b_ln_b[j]),
                   ev_w_out[j].astype(BF16))
            zeros = jnp.zeros((BATCH, CV_PAD, CV_WIDTH), F32)
            xf, cp = _even_call(xf, zeros, *prm, emit_v=False, **prompt)
            xf, cs, vs = _even_call(xf, pad_hist(state_conv[j]), *prm, emit_v=True, **sample)
            conv_p.append(cp[:, CV_PAD - CV_STATE:])
            conv_s.append(cs[:, CV_PAD - CV_STATE:])
            v_s.append(vs)
        else:
            prm = (row(norm_mix[l]), od_lower, od_w_in[j].astype(BF16), row(od_norm_g[j]), od_w_out[j].astype(BF16))
            zeros = jnp.zeros((BATCH, HG_HEADS, HG_DK, HG_DK), F32)
            xf, sp = _odd_call(xf, zeros, *prm, layer=l, **prompt)
            xf, ss = _odd_call(xf, state_hgrn[j], *prm, layer=l, **sample)
            hg_p.append(sp)
            hg_s.append(ss)
        nf = row(norm_ffn[l])
        keys = peer_keys[l].reshape(2 * PK_HEADS, PK_NKEYS, PK_HALF).astype(BF16)
        eidx, gate = _peer_select(xf, nf, peer_wq[l].astype(BF16), keys)
        w = _peer_u(eidx, xf, nf, gate, _pack_table(peer_u[l]))
        xf = _peer_v(eidx, w, xf, _pack_table(peer_v[l]))
    g = row(norm_final)
    y_prompt = _final_call(xf, g, n_rows=N_PROMPT, blk0=0).reshape(BATCH, SEQ, D_MODEL)
    y_sample = _final_call(xf, g, n_rows=N_SAMPLE, blk0=N_PROMPT // MIX_TB).reshape(DEC_BATCH, DEC_SEQ, D_MODEL)
    return (y_prompt, y_sample, jnp.stack(conv_p), jnp.stack(conv_s), jnp.stack(hg_p), jnp.stack(hg_s),
            jnp.stack(v_s))
```

```python
import functools

import jax
import jax.numpy as jnp
from jax import lax
from jax.experimental import pallas as pl
from jax.experimental.pallas import tpu as pltpu

F32 = jnp.float32
BF16 = jnp.bfloat16
HIGHEST = lax.Precision.HIGHEST

SUB = 8
LANE = 128

D_MODEL = 1024
BATCH = 2
SEQ = 16384
DEPTH = 4
DEC_BATCH = 8
DEC_SEQ = 32
N_PROMPT = BATCH * SEQ
N_SAMPLE = DEC_BATCH * DEC_SEQ
N_TOK = N_PROMPT + N_SAMPLE
EPS = 1e-6

GM_WIDTH = D_MODEL // 2
GM_GROUPS = 4
GM_GROUP_DIM = GM_WIDTH // GM_GROUPS
GM_CHUNK = 128
CV_WIDTH = D_MODEL // 2
CV_KERNEL = 31
CV_STATE = CV_KERNEL - 1
CV_PAD = 32
HG_HEADS = 8
HG_DK = D_MODEL // HG_HEADS
HG_CHUNK = 64
PK_HEADS = 8
PK_NKEYS = 128
PK_EXPERTS = PK_NKEYS * PK_NKEYS
PK_HALF = 128
PK_TOPK = 16
PK_PAIRS = PK_HEADS * PK_TOPK
HALF = D_MODEL // 2
HSUB = HALF // LANE
STAGE_STRIDE = 136

MIX_TB = 256
SEL_TB = 256
PEER_TB = 128
VMEM_LIMIT = 48 * 1024 * 1024


def _gelu_tanh(x):
    return 0.5 * x * (1.0 + jnp.tanh(0.7978845608028654 * (x + 0.044715 * x * x * x)))


def _silu(x):
    return x * jax.nn.sigmoid(x)


def _rms(x, g):
    return x * lax.rsqrt(jnp.mean(x * x, axis=-1, keepdims=True) + EPS) * g


def _ln(x, g, b):
    xc = x - jnp.mean(x, axis=-1, keepdims=True)
    return xc * lax.rsqrt(jnp.mean(xc * xc, axis=-1, keepdims=True) + EPS) * g + b


def _bdot(a, b):
    return jnp.dot(a.astype(BF16), b.astype(BF16), preferred_element_type=F32)


def _even_kernel(emit_v, x_ref, prev_ref, nm_ref, win_ref, ag_ref, ab_ref, ws_ref, bst_ref, cw_ref, cb_ref,
                 bg_ref, bb_ref, wout_ref, o_ref, cst_ref, *rest):
    if emit_v:
        v_ref, pbuf = rest
    else:
        (pbuf,) = rest
    tb = x_ref.shape[0]
    gl = min(tb, GM_CHUNK)

    @pl.when(pl.program_id(1) == 0)
    def _():
        pbuf[0:CV_PAD, :] = prev_ref[0]

    x = x_ref[...]
    z = _bdot(_rms(x, nm_ref[...]), win_ref[...])
    za = _gelu_tanh(z[:, :2 * GM_WIDTH])
    u = za[:, :GM_WIDTH]
    v = _ln(za[:, GM_WIDTH:], ag_ref[...], ab_ref[...])
    if emit_v:
        v_ref[0] = v

    row = lax.broadcasted_iota(jnp.int32, (gl, gl), 0)
    col = lax.broadcasted_iota(jnp.int32, (gl, gl), 1)
    wts = [jnp.where(row >= col, ws_ref[g, 0:gl, 0:gl], 0.0) for g in range(GM_GROUPS)]
    chunks = []
    for c in range(tb // gl):
        vc = v[c * gl:(c + 1) * gl]
        parts = [jnp.dot(wts[g], vc[:, g * GM_GROUP_DIM:(g + 1) * GM_GROUP_DIM], precision=HIGHEST,
                         preferred_element_type=F32) + bst_ref[0:gl, g:g + 1] for g in range(GM_GROUPS)]
        chunks.append(jnp.concatenate(parts, axis=1))
    a_out = u * jnp.concatenate(chunks, axis=0)

    zb = z[:, 2 * GM_WIDTH:]
    pbuf[CV_PAD:CV_PAD + tb, :] = zb[:, :CV_WIDTH] * jax.nn.sigmoid(zb[:, CV_WIDTH:])
    first = CV_PAD - CV_STATE
    conv = cb_ref[...] + cw_ref[0:1, :] * pbuf[first:first + tb, :]
    for j in range(1, CV_KERNEL):
        conv = conv + cw_ref[j:j + 1, :] * pbuf[first + j:first + j + tb, :]
    b_out = _silu(_ln(conv, bg_ref[...], bb_ref[...]))

    o_ref[...] = x + _bdot(jnp.concatenate([a_out, b_out], axis=1), wout_ref[...])
    tail = pbuf[tb:tb + CV_PAD, :]
    cst_ref[0] = tail
    pbuf[0:CV_PAD, :] = tail


def _even_call(xf, prev, nm, win, ag, ab, ws, bst, cw, cb, bg, bb, wout, *, nb, t_len, tb, blk0, emit_v):
    nt = t_len // tb
    whole = lambda a: pl.BlockSpec(a.shape, lambda b, t: (0,) * a.ndim)
    x_spec = pl.BlockSpec((tb, D_MODEL), lambda b, t: (blk0 + b * nt + t, 0))
    params = (nm, win, ag, ab, ws, bst, cw, cb, bg, bb, wout)
    out_shape = [jax.ShapeDtypeStruct(xf.shape, F32), jax.ShapeDtypeStruct((nb, CV_PAD, CV_WIDTH), F32)]
    out_specs = [x_spec, pl.BlockSpec((1, CV_PAD, CV_WIDTH), lambda b, t: (b, 0, 0))]
    if emit_v:
        out_shape.append(jax.ShapeDtypeStruct((nb, t_len, GM_WIDTH), F32))
        out_specs.append(pl.BlockSpec((1, tb, GM_WIDTH), lambda b, t: (b, t, 0)))
    return pl.pallas_call(
        functools.partial(_even_kernel, emit_v),
        grid=(nb, nt),
        in_specs=[x_spec, pl.BlockSpec((1, CV_PAD, CV_WIDTH), lambda b, t: (b, 0, 0))] + [whole(a) for a in params],
        out_specs=out_specs,
        out_shape=out_shape,
        scratch_shapes=[pltpu.VMEM((CV_PAD + tb, CV_WIDTH), F32)],
        input_output_aliases={0: 0},
        compiler_params=pltpu.CompilerParams(
            dimension_semantics=("arbitrary", "arbitrary"), vmem_limit_bytes=VMEM_LIMIT),
        name="even_mixer",
    )(xf, prev, *params)


def _odd_kernel(layer, x_ref, s0_ref, nm_ref, lower_ref, win_ref, ng_ref, wout_ref, o_ref, sout_ref,
                st_s, q_s, k_s, v_s, b_s, o_s):
    tb = x_ref.shape[0]
    cl = min(tb, HG_CHUNK)
    nck = tb // cl
    t = pl.program_id(1)

    @pl.when(t == 0)
    def _():
        for hh in range(HG_HEADS):
            st_s[hh] = s0_ref[0, hh].T

    x = x_ref[...]
    z = _bdot(_rms(x, nm_ref[...]), win_ref[...])
    low = lower_ref[...]
    ex = jnp.exp(low - jnp.max(low, axis=0, keepdims=True))
    soft = ex / jnp.sum(ex, axis=0, keepdims=True)
    lb = jnp.sum(soft[1:layer + 1], axis=0, keepdims=True)
    zf = z[:, D_MODEL:2 * D_MODEL]
    q = _silu(z[:, :D_MODEL])
    logf = jnp.log(lb + (1.0 - lb) * jax.nn.sigmoid(zf))
    k = (1.0 - lb) * jax.nn.sigmoid(-zf)
    vv = z[:, 2 * D_MODEL:3 * D_MODEL]

    row = lax.broadcasted_iota(jnp.int32, (cl, cl), 0)
    col = lax.broadcasted_iota(jnp.int32, (cl, cl), 1)
    tril = (row >= col).astype(F32)
    for c in range(nck):
        bc = jnp.dot(tril, logf[c * cl:(c + 1) * cl], precision=HIGHEST, preferred_element_type=F32)
        for hh in range(HG_HEADS):
            b_s[hh, c * cl:(c + 1) * cl, :] = bc[:, hh * HG_DK:(hh + 1) * HG_DK]
    for hh in range(HG_HEADS):
        sl = slice(hh * HG_DK, (hh + 1) * HG_DK)
        q_s[hh] = q[:, sl]
        k_s[hh] = k[:, sl]
        v_s[hh] = vv[:, sl]

    nblk = cl // SUB
    rowi = lax.broadcasted_iota(jnp.int32, (SUB, HG_DK), 0)

    def chunk_head(i, carry):
        c = i // HG_HEADS
        hh = i % HG_HEADS
        rows = pl.ds(pl.multiple_of(c * cl, cl), cl)
        qh = q_s[hh, rows, :]
        kh = k_s[hh, rows, :]
        vh = v_s[hh, rows, :]
        bh = b_s[hh, rows, :]
        st = st_s[hh]
        o = lax.dot_general((qh * jnp.exp(bh)).astype(BF16), st.astype(BF16), (((1,), (1,)), ((), ())),
                            preferred_element_type=F32)
        qb = [qh[SUB * r:SUB * (r + 1)] for r in range(nblk)]
        bb = [bh[SUB * r:SUB * (r + 1)] for r in range(nblk)]
        ob = [o[SUB * r:SUB * (r + 1)] for r in range(nblk)]
        for s in range(cl):
            b_src = bh[s:s + 1]
            k_src = kh[s:s + 1]
            v_src = vh[s:s + 1]
            for r in range(s // SUB, nblk):
                rel = bb[r] - b_src
                if r == s // SUB:
                    rel = jnp.where(rowi >= (s % SUB), rel, -jnp.inf)
                att = jnp.sum(qb[r] * jnp.exp(rel) * k_src, axis=-1, keepdims=True)
                ob[r] = ob[r] + att * v_src
        o_s[hh, rows, :] = jnp.concatenate(ob, axis=0)
        b_last = bh[cl - 1:cl]
        kt = kh * jnp.exp(b_last - bh)
        upd = lax.dot_general(vh.astype(BF16), kt.astype(BF16), (((0,), (0,)), ((), ())),
                              preferred_element_type=F32)
        st_s[hh] = st * jnp.exp(b_last) + upd
        return carry

    lax.fori_loop(0, nck * HG_HEADS, chunk_head, 0)

    parts = []
    for hh in range(HG_HEADS):
        oh = o_s[hh]
        on = oh * lax.rsqrt(jnp.mean(oh * oh, axis=-1, keepdims=True) + EPS) * ng_ref[...]
        parts.append(on * _silu(z[:, 3 * D_MODEL + hh * HG_DK:3 * D_MODEL + (hh + 1) * HG_DK]))
    o_ref[...] = x + _bdot(jnp.concatenate(parts, axis=1), wout_ref[...])

    @pl.when(t == pl.num_programs(1) - 1)
    def _():
        for hh in range(HG_HEADS):
            sout_ref[0, hh] = st_s[hh].T


def _odd_call(xf, s0, nm, lower, win, ng, wout, *, layer, nb, t_len, tb, blk0):
    nt = t_len // tb
    whole = lambda a: pl.BlockSpec(a.shape, lambda b, t: (0,) * a.ndim)
    x_spec = pl.BlockSpec((tb, D_MODEL), lambda b, t: (blk0 + b * nt + t, 0))
    s_spec = pl.BlockSpec((1, HG_HEADS, HG_DK, HG_DK), lambda b, t: (b, 0, 0, 0))
    params = (nm, lower, win, ng, wout)
    head_buf = pltpu.VMEM((HG_HEADS, tb, HG_DK), F32)
    return pl.pallas_call(
        functools.partial(_odd_kernel, layer),
        grid=(nb, nt),
        in_specs=[x_spec, s_spec] + [whole(a) for a in params],
        out_specs=[x_spec, s_spec],
        out_shape=[jax.ShapeDtypeStruct(xf.shape, F32),
                   jax.ShapeDtypeStruct((nb, HG_HEADS, HG_DK, HG_DK), F32)],
        scratch_shapes=[pltpu.VMEM((HG_HEADS, HG_DK, HG_DK), F32), head_buf, head_buf, head_buf, head_buf, head_buf],
        input_output_aliases={0: 0},
        compiler_params=pltpu.CompilerParams(
            dimension_semantics=("arbitrary", "arbitrary"), vmem_limit_bytes=VMEM_LIMIT),
        name="odd_mixer",
    )(xf, s0, *params)


def _candidate_tiles():
    tiles = [
        [(0, b) for b in range(8)],
        [(0, b) for b in range(8, 16)],
        [(1, b) for b in range(8)],
        [(2, b) for b in range(5)] + [(5, 0), (5, 1), (8, 0)],
        [(3, b) for b in range(4)] + [(4, b) for b in range(3)] + [(9, 0)],
        [(6, 0), (6, 1), (7, 0), (7, 1), (10, 0), (11, 0), (12, 0), (13, 0)],
        [(14, 0), (15, 0), (2, 5), (2, 6), (2, 7), (3, 4), (3, 5), (3, 6)],
    ]
    flat = [ab for tile in tiles for ab in tile]
    need = [(a, b) for a in range(PK_TOPK) for b in range(PK_TOPK) if (a + 1) * (b + 1) <= PK_TOPK]
    assert len(set(flat)) == len(flat) and set(need) <= set(flat) and all(len(t) == SUB for t in tiles)
    return tiles


def _peer_select_kernel(x_ref, nf_ref, wq_ref, keys_ref, eidx_ref, gate_ref, st_s, tv_s, ti_s, es_s, gs_s):
    tb = x_ref.shape[0]
    nlt = tb // LANE
    q = _bdot(_rms(x_ref[...], nf_ref[...]), wq_ref[...])
    for hp in range(2 * PK_HEADS):
        sc = lax.dot_general(keys_ref[hp], q[:, hp * PK_HALF:(hp + 1) * PK_HALF].astype(BF16),
                             (((1,), (1,)), ((), ())), preferred_element_type=F32)
        for lt in range(nlt):
            st_s[hp, lt] = sc[:, lt * LANE:(lt + 1) * LANE]

    kiota = lax.broadcasted_iota(jnp.int32, (PK_NKEYS, LANE), 0)
    rowi = lax.broadcasted_iota(jnp.int32, (SUB, LANE), 0)
    tiles = _candidate_tiles()
    pos_tiles = []
    for tile in tiles:
        p = jnp.full((SUB, LANE), tile[0][0] * PK_TOPK + tile[0][1], jnp.int32)
        for r in range(1, SUB):
            p = jnp.where(rowi >= r, tile[r][0] * PK_TOPK + tile[r][1], p)
        pos_tiles.append(p)
    pos = jnp.concatenate(pos_tiles, axis=0)
    big = PK_TOPK * PK_TOPK

    def pair_rows(tile, first, second):
        fa = sb = None
        rolled = {}
        for r, (a, b) in enumerate(tile):
            if r == 0 or a != tile[r - 1][0]:
                row = jnp.broadcast_to(first[a:a + 1], (SUB, LANE))
                fa = row if r == 0 else jnp.where(rowi >= r, row, fa)
            key = (b // SUB, (r - b) % SUB)
            if r == 0 or key != prev_key:
                if key not in rolled:
                    src = second[key[0] * SUB:(key[0] + 1) * SUB]
                    rolled[key] = src if key[1] == 0 else pltpu.roll(src, key[1], 0)
                sb = rolled[key] if r == 0 else jnp.where(rowi >= r, rolled[key], sb)
            prev_key = key
        return fa, sb

    streams1 = [(p, lt) for lt in range(nlt) for p in range(2)]

    def head(hd, carry):
        ss = [st_s[2 * hd + p, lt] for p, lt in streams1]
        for r in range(PK_TOPK):
            for n, (p, lt) in enumerate(streams1):
                s = ss[n]
                m = jnp.max(s, axis=0, keepdims=True)
                ix = jnp.min(jnp.where(s == m, kiota, PK_NKEYS), axis=0, keepdims=True)
                tv_s[2 * hd + p, lt, r:r + 1, :] = m
                ti_s[2 * hd + p, lt, r:r + 1, :] = ix
                ss[n] = jnp.where(kiota == ix, -jnp.inf, s)
        cands, cidxs, bests = [], [], [[] for _ in range(nlt)]
        for lt in range(nlt):
            s1 = tv_s[2 * hd, lt]
            i1 = ti_s[2 * hd, lt]
            s2 = tv_s[2 * hd + 1, lt]
            i2 = ti_s[2 * hd + 1, lt]
            sums, ids = [], []
            for tile in tiles:
                va, vb = pair_rows(tile, s1, s2)
                ia, ib = pair_rows(tile, i1, i2)
                sums.append(va + vb)
                ids.append(ia * PK_NKEYS + ib)
            cands.append(jnp.concatenate(sums, axis=0))
            cidxs.append(jnp.concatenate(ids, axis=0))
        for r in range(PK_TOPK):
            for lt in range(nlt):
                cand = cands[lt]
                m = jnp.max(cand, axis=0, keepdims=True)
                pick = jnp.min(jnp.where(cand == m, pos, big), axis=0, keepdims=True)
                sel = pos == pick
                es_s[lt, hd, r:r + 1, :] = jnp.max(jnp.where(sel, cidxs[lt], -1), axis=0,
                                                   keepdims=True).astype(F32)
                bests[lt].append(m)
                cands[lt] = jnp.where(sel, -jnp.inf, cand)
        for lt in range(nlt):
            ex = [jnp.exp(b - bests[lt][0]) for b in bests[lt]]
            den = ex[0]
            for r in range(1, PK_TOPK):
                den = den + ex[r]
            for r in range(PK_TOPK):
                gs_s[lt, hd, r:r + 1, :] = ex[r] / den
        return carry

    lax.fori_loop(0, PK_HEADS, head, 0)
    for lt in range(nlt):
        rows = slice(lt * LANE, (lt + 1) * LANE)
        eidx_ref[:, rows] = es_s[lt].reshape(PK_PAIRS, LANE).astype(jnp.int32)
        gate_ref[rows, :] = gs_s[lt].reshape(PK_PAIRS, LANE).T


def _peer_select(xf, nf, wq, keys):
    n_tok = xf.shape[0]
    nlt = SEL_TB // LANE
    whole = lambda a: pl.BlockSpec(a.shape, lambda i: (0,) * a.ndim)
    return pl.pallas_call(
        _peer_select_kernel,
        grid=(n_tok // SEL_TB,),
        in_specs=[pl.BlockSpec((SEL_TB, D_MODEL), lambda i: (i, 0)), whole(nf), whole(wq), whole(keys)],
        out_specs=[pl.BlockSpec((PK_PAIRS, SEL_TB), lambda i: (0, i)),
                   pl.BlockSpec((SEL_TB, PK_PAIRS), lambda i: (i, 0))],
        out_shape=[jax.ShapeDtypeStruct((PK_PAIRS, n_tok), jnp.int32),
                   jax.ShapeDtypeStruct((n_tok, PK_PAIRS), F32)],
        scratch_shapes=[pltpu.VMEM((2 * PK_HEADS, nlt, PK_NKEYS, LANE), F32),
                        pltpu.VMEM((2 * PK_HEADS, nlt, PK_TOPK, LANE), F32),
                        pltpu.VMEM((2 * PK_HEADS, nlt, PK_TOPK, LANE), jnp.int32),
                        pltpu.VMEM((nlt, PK_HEADS, PK_TOPK, LANE), F32),
                        pltpu.VMEM((nlt, PK_HEADS, PK_TOPK, LANE), F32)],
        compiler_params=pltpu.CompilerParams(
            dimension_semantics=("arbitrary",), vmem_limit_bytes=VMEM_LIMIT),
        name="peer_select",
    )(xf, nf, wq, keys)


def _pack_table(tab):
    b = lax.bitcast_convert_type(tab.astype(BF16), jnp.uint16).astype(jnp.uint32)
    return (b[:, :HALF] | (b[:, HALF:] << 16)).reshape(PK_EXPERTS, HSUB, LANE)


N_PIECE = 2


def _split(x):
    p1 = x.astype(BF16).astype(F32)
    p2 = (x - p1).astype(BF16).astype(F32)
    return p1, p2


def _index_copies(idx_hbm, idx_smem, sems):
    first = pl.program_id(0) * PEER_TB
    return [pltpu.make_async_copy(idx_hbm.at[k, pl.ds(first, PEER_TB)], idx_smem[k], sems.at[k])
            for k in range(PK_PAIRS)]


def _gather_rows(tab_ref, idx_smem, t, stage):
    for k in range(PK_PAIRS):
        stage[pl.ds(k, HSUB, stride=STAGE_STRIDE), :] = tab_ref[idx_smem[k][t]]


def _stage_block(stage, s):
    return pltpu.bitcast(stage[s * STAGE_STRIDE:s * STAGE_STRIDE + PK_PAIRS, :], BF16)


PIPE_TOK = 2


def _token_pipeline(tab_ref, idx_smem, stages, compute):
    sets = (stages[:PIPE_TOK], stages[PIPE_TOK:])
    for j in range(PIPE_TOK):
        _gather_rows(tab_ref, idx_smem, j, sets[0][j])

    def step(t, cur, nxt):
        for j in range(PIPE_TOK):
            _gather_rows(tab_ref, idx_smem, jnp.minimum(t + PIPE_TOK + j, PEER_TB - 1), nxt[j])
        for j in range(PIPE_TOK):
            compute(t + j, cur[j])

    def body(i, carry):
        @pl.when(i % 2 == 0)
        def _():
            step(PIPE_TOK * i, sets[0], sets[1])

        @pl.when(i % 2 == 1)
        def _():
            step(PIPE_TOK * i, sets[1], sets[0])

        return carry

    lax.fori_loop(0, PEER_TB // PIPE_TOK, body, 0)


def _peer_u_kernel(idx_hbm, x_ref, nf_ref, gate_ref, tab_ref, w_ref, sems, hs, ybuf, *bufs):
    stages, idx_smem = bufs[:2 * PIPE_TOK], bufs[2 * PIPE_TOK:]
    copies = _index_copies(idx_hbm, idx_smem, sems)
    for cp in copies:
        cp.start()
    pieces = _split(_rms(x_ref[...], nf_ref[...]))
    hs[:, :, 2 * N_PIECE:, :] = jnp.zeros((PEER_TB, HSUB, SUB - 2 * N_PIECE, LANE), F32)
    for i in range(N_PIECE):
        for s in range(HSUB):
            hs[:, s, i, :] = pieces[i][:, s * LANE:(s + 1) * LANE]
            hs[:, s, N_PIECE + i, :] = pieces[i][:, HALF + s * LANE:HALF + (s + 1) * LANE]
    for cp in copies:
        cp.wait()
    rowi = lax.broadcasted_iota(jnp.int32, (SUB, 2 * PK_PAIRS), 0)
    odd = lax.broadcasted_iota(jnp.int32, (SUB, 2 * PK_PAIRS), 1) % 2
    keep = jnp.where(odd == 0, (rowi < N_PIECE).astype(F32),
                     ((rowi >= N_PIECE) & (rowi < 2 * N_PIECE)).astype(F32))

    def compute(t, stage):
        acc = jnp.zeros((SUB, 2 * PK_PAIRS), F32)
        for s in range(HSUB):
            acc = acc + lax.dot_general(hs[t, s].astype(BF16), _stage_block(stage, s), (((1,), (1,)), ((), ())),
                                        preferred_element_type=F32)
        ybuf[pl.ds(t, 1), :] = jnp.sum(acc * keep, axis=0, keepdims=True)

    _token_pipeline(tab_ref, idx_smem, stages, compute)
    r = lax.broadcasted_iota(jnp.int32, (2 * PK_PAIRS, PK_PAIRS), 0)
    c = lax.broadcasted_iota(jnp.int32, (2 * PK_PAIRS, PK_PAIRS), 1)
    fold = (r // 2 == c).astype(F32)
    act = jnp.dot(ybuf[...], fold, precision=HIGHEST, preferred_element_type=F32)
    w_ref[...] = gate_ref[...] * _gelu_tanh(act)


def _peer_u(eidx, xf, nf, gate, tab_packed):
    n_tok = xf.shape[0]
    stages = [pltpu.VMEM((HSUB * STAGE_STRIDE, LANE), jnp.uint32)] * (2 * PIPE_TOK)
    return pl.pallas_call(
        _peer_u_kernel,
        grid=(n_tok // PEER_TB,),
        in_specs=[
            pl.BlockSpec(memory_space=pl.ANY),
            pl.BlockSpec((PEER_TB, D_MODEL), lambda i: (i, 0)),
            pl.BlockSpec(nf.shape, lambda i: (0, 0)),
            pl.BlockSpec((PEER_TB, PK_PAIRS), lambda i: (i, 0)),
            pl.BlockSpec(memory_space=pltpu.VMEM),
        ],
        out_specs=pl.BlockSpec((PEER_TB, PK_PAIRS), lambda i: (i, 0)),
        out_shape=jax.ShapeDtypeStruct((n_tok, PK_PAIRS), F32),
        scratch_shapes=[
            pltpu.SemaphoreType.DMA((PK_PAIRS,)),
            pltpu.VMEM((PEER_TB, HSUB, SUB, LANE), F32),
            pltpu.VMEM((PEER_TB, 2 * PK_PAIRS), F32),
        ] + stages + [pltpu.SMEM((PEER_TB,), jnp.int32)] * PK_PAIRS,
        compiler_params=pltpu.CompilerParams(
            dimension_semantics=("arbitrary",), vmem_limit_bytes=VMEM_LIMIT),
        name="peer_u",
    )(eidx, xf, nf, gate, tab_packed)


def _peer_v_kernel(idx_hbm, w_ref, x_ref, tab_ref, o_ref, sems, av, ov, *bufs):
    stages, idx_smem = bufs[:2 * PIPE_TOK], bufs[2 * PIPE_TOK:]
    copies = _index_copies(idx_hbm, idx_smem, sems)
    for cp in copies:
        cp.start()
    r = lax.broadcasted_iota(jnp.int32, (PK_PAIRS, 2 * PK_PAIRS), 0)
    c = lax.broadcasted_iota(jnp.int32, (PK_PAIRS, 2 * PK_PAIRS), 1)
    spread = [(c == 2 * r).astype(BF16), (c == 2 * r + 1).astype(BF16)]
    pieces = _split(w_ref[...])
    av[:, 2 * N_PIECE:, :] = jnp.zeros((PEER_TB, SUB - 2 * N_PIECE, 2 * PK_PAIRS), F32)
    for i in range(N_PIECE):
        for j in range(2):
            av[:, N_PIECE * j + i, :] = jnp.dot(pieces[i].astype(BF16), spread[j], preferred_element_type=F32)
    for cp in copies:
        cp.wait()

    def compute(t, stage):
        wt = av[t].astype(BF16)
        lo, hi = [], []
        for s in range(HSUB):
            y = jnp.dot(wt, _stage_block(stage, s), preferred_element_type=F32)
            lo.append(y[0:1] + y[1:2])
            hi.append(y[2:3] + y[3:4])
        ov[pl.ds(t, 1), :] = jnp.concatenate(lo + hi, axis=1)

    _token_pipeline(tab_ref, idx_smem, stages, compute)
    o_ref[...] = x_ref[...] + ov[...]


def _peer_v(eidx, w, xf, tab_packed):
    n_tok = xf.shape[0]
    stages = [pltpu.VMEM((HSUB * STAGE_STRIDE, LANE), jnp.uint32)] * (2 * PIPE_TOK)
    return pl.pallas_call(
        _peer_v_kernel,
        grid=(n_tok // PEER_TB,),
        in_specs=[
            pl.BlockSpec(memory_space=pl.ANY),
            pl.BlockSpec((PEER_TB, PK_PAIRS), lambda i: (i, 0)),
            pl.BlockSpec((PEER_TB, D_MODEL), lambda i: (i, 0)),
            pl.BlockSpec(memory_space=pltpu.VMEM),
        ],
        out_specs=pl.BlockSpec((PEER_TB, D_MODEL), lambda i: (i, 0)),
        out_shape=jax.ShapeDtypeStruct((n_tok, D_MODEL), F32),
        scratch_shapes=[
            pltpu.SemaphoreType.DMA((PK_PAIRS,)),
            pltpu.VMEM((PEER_TB, SUB, 2 * PK_PAIRS), F32),
            pltpu.VMEM((PEER_TB, D_MODEL), F32),
        ] + stages + [pltpu.SMEM((PEER_TB,), jnp.int32)] * PK_PAIRS,
        compiler_params=pltpu.CompilerParams(
            dimension_semantics=("arbitrary",), vmem_limit_bytes=VMEM_LIMIT),
        name="peer_v",
    )(eidx, w, xf, tab_packed)


def _final_kernel(x_ref, g_ref, o_ref):
    o_ref[...] = _rms(x_ref[...], g_ref[...])


def _final_call(xf, g, *, n_rows, blk0):
    nblk = n_rows // MIX_TB
    return pl.pallas_call(
        _final_kernel,
        grid=(nblk,),
        in_specs=[pl.BlockSpec((MIX_TB, D_MODEL), lambda i: (blk0 + i, 0)), pl.BlockSpec(g.shape, lambda i: (0, 0))],
        out_specs=pl.BlockSpec((MIX_TB, D_MODEL), lambda i: (i, 0)),
        out_shape=jax.ShapeDtypeStruct((n_rows, D_MODEL), F32),
        compiler_params=pltpu.CompilerParams(dimension_semantics=("arbitrary",)),
        name="final_norm",
    )(xf, g)


def kernel(x_prompt, x_sample, state_conv, state_hgrn, norm_mix, norm_ffn, norm_final, ev_w_in, ev_a_ln_g, ev_a_ln_b, ev_ws, ev_bs, ev_conv_w, ev_conv_b, ev_b_ln_g, ev_b_ln_b, ev_w_out, od_w_in, od_lower, od_norm_g, od_w_out, peer_wq, peer_keys, peer_u, peer_v):
    row = lambda a: a.reshape(1, -1)
    xf = jnp.concatenate([x_prompt.reshape(N_PROMPT, D_MODEL), x_sample.reshape(N_SAMPLE, D_MODEL)], axis=0)
    pad_hist = lambda a: jnp.pad(a, ((0, 0), (CV_PAD - CV_STATE, 0), (0, 0)))
    conv_p, conv_s, hg_p, hg_s, v_s = [], [], [], [], []
    prompt = dict(nb=BATCH, t_len=SEQ, tb=MIX_TB, blk0=0)
    sample = dict(nb=DEC_BATCH, t_len=DEC_SEQ, tb=DEC_SEQ, blk0=N_PROMPT // DEC_SEQ)
    for l in range(DEPTH):
        j = l // 2
        if l % 2 == 0:
            prm = (row(norm_mix[l]), ev_w_in[j].astype(BF16), row(ev_a_ln_g[j]), row(ev_a_ln_b[j]), ev_ws[j],
                   ev_bs[j].T, ev_conv_w[j], row(ev_conv_b[j]), row(ev_b_ln_g[j]), row(ev_b_ln_b[j]),
                   ev_w_out[j].astype(BF16))
            zeros = jnp.zeros((BATCH, CV_PAD, CV_WIDTH), F32)
            xf, cp = _even_call(xf, zeros, *prm, emit_v=False, **prompt)
            xf, cs, vs = _even_call(xf, pad_hist(state_conv[j]), *prm, emit_v=True, **sample)
            conv_p.append(cp[:, CV_PAD - CV_STATE:])
            conv_s.append(cs[:, CV_PAD - CV_STATE:])
            v_s.append(vs)
        else:
            prm = (row(norm_mix[l]), od_lower, od_w_in[j].astype(BF16), row(od_norm_g[j]), od_w_out[j].astype(BF16))
            zeros = jnp.zeros((BATCH, HG_HEADS, HG_DK, HG_DK), F32)
            xf, sp = _odd_call(xf, zeros, *prm, layer=l, **prompt)
            xf, ss = _odd_call(xf, state_hgrn[j], *prm, layer=l, **sample)
            hg_p.append(sp)
            hg_s.append(ss)
        nf = row(norm_ffn[l])
        keys = peer_keys[l].reshape(2 * PK_HEADS, PK_NKEYS, PK_HALF).astype(BF16)
        eidx, gate = _peer_select(xf, nf, peer_wq[l].astype(BF16), keys)
        w = _peer_u(eidx, xf, nf, gate, _pack_table(peer_u[l]))
        xf = _peer_v(eidx, w, xf, _pack_table(peer_v[l]))
    g = row(norm_final)
    y_prompt = _final_call(xf, g, n_rows=N_PROMPT, blk0=0).reshape(BATCH, SEQ, D_MODEL)
    y_sample = _final_call(xf, g, n_rows=N_SAMPLE, blk0=N_PROMPT // MIX_TB).reshape(DEC_BATCH, DEC_SEQ, D_MODEL)
    return (y_prompt, y_sample, jnp.stack(conv_p), jnp.stack(conv_s), jnp.stack(hg_p), jnp.stack(hg_s),
            jnp.stack(v_s))
```

```python
import functools

import jax
import jax.numpy as jnp
from jax import lax
from jax.experimental import pallas as pl
from jax.experimental.pallas import tpu as pltpu

F32 = jnp.float32
BF16 = jnp.bfloat16
HIGHEST = lax.Precision.HIGHEST

SUB = 8
LANE = 128

D_MODEL = 1024
BATCH = 2
SEQ = 16384
DEPTH = 4
DEC_BATCH = 8
DEC_SEQ = 32
N_PROMPT = BATCH * SEQ
N_SAMPLE = DEC_BATCH * DEC_SEQ
N_TOK = N_PROMPT + N_SAMPLE
EPS = 1e-6

GM_WIDTH = D_MODEL // 2
GM_GROUPS = 4
GM_GROUP_DIM = GM_WIDTH // GM_GROUPS
GM_CHUNK = 128
CV_WIDTH = D_MODEL // 2
CV_KERNEL = 31
CV_STATE = CV_KERNEL - 1
CV_PAD = 32
HG_HEADS = 8
HG_DK = D_MODEL // HG_HEADS
HG_CHUNK = 64
HG_SUB = 16
PK_HEADS = 8
PK_NKEYS = 128
PK_EXPERTS = PK_NKEYS * PK_NKEYS
PK_HALF = 128
PK_TOPK = 16
PK_PAIRS = PK_HEADS * PK_TOPK
HALF = D_MODEL // 2
HSUB = HALF // LANE
STAGE_STRIDE = 136

MIX_TB = 256
SEL_TB = 256
PEER_TB = 128
PACK_ROWS = 512
VMEM_LIMIT = 48 * 1024 * 1024


def _gelu_tanh(x):
    return 0.5 * x * (1.0 + jnp.tanh(0.7978845608028654 * (x + 0.044715 * x * x * x)))


def _silu(x):
    return x * jax.nn.sigmoid(x)


def _rms(x, g):
    return x * lax.rsqrt(jnp.mean(x * x, axis=-1, keepdims=True) + EPS) * g


def _ln(x, g, b):
    xc = x - jnp.mean(x, axis=-1, keepdims=True)
    return xc * lax.rsqrt(jnp.mean(xc * xc, axis=-1, keepdims=True) + EPS) * g + b


def _bdot(a, b):
    return jnp.dot(a.astype(BF16), b.astype(BF16), preferred_element_type=F32)


def _even_kernel(emit_v, x_ref, prev_ref, nm_ref, win_ref, ag_ref, ab_ref, ws_ref, bst_ref, cw_ref, cb_ref,
                 bg_ref, bb_ref, wout_ref, o_ref, cst_ref, *rest):
    if emit_v:
        v_ref, pbuf = rest
    else:
        (pbuf,) = rest
    tb = x_ref.shape[0]
    gl = min(tb, GM_CHUNK)

    @pl.when(pl.program_id(1) == 0)
    def _():
        pbuf[0:CV_PAD, :] = prev_ref[0]

    x = x_ref[...]
    z = _bdot(_rms(x, nm_ref[...]), win_ref[...])
    za = _gelu_tanh(z[:, :2 * GM_WIDTH])
    u = za[:, :GM_WIDTH]
    v = _ln(za[:, GM_WIDTH:], ag_ref[...], ab_ref[...])
    if emit_v:
        v_ref[0] = v

    row = lax.broadcasted_iota(jnp.int32, (gl, gl), 0)
    col = lax.broadcasted_iota(jnp.int32, (gl, gl), 1)
    wts = [jnp.where(row >= col, ws_ref[g, 0:gl, 0:gl], 0.0) for g in range(GM_GROUPS)]
    chunks = []
    for c in range(tb // gl):
        vc = v[c * gl:(c + 1) * gl]
        parts = [jnp.dot(wts[g], vc[:, g * GM_GROUP_DIM:(g + 1) * GM_GROUP_DIM], precision=HIGHEST,
                         preferred_element_type=F32) + bst_ref[0:gl, g:g + 1] for g in range(GM_GROUPS)]
        chunks.append(jnp.concatenate(parts, axis=1))
    a_out = u * jnp.concatenate(chunks, axis=0)

    zb = z[:, 2 * GM_WIDTH:]
    pbuf[CV_PAD:CV_PAD + tb, :] = zb[:, :CV_WIDTH] * jax.nn.sigmoid(zb[:, CV_WIDTH:])
    first = CV_PAD - CV_STATE
    conv = cb_ref[...] + cw_ref[0:1, :] * pbuf[first:first + tb, :]
    for j in range(1, CV_KERNEL):
        conv = conv + cw_ref[j:j + 1, :] * pbuf[first + j:first + j + tb, :]
    b_out = _silu(_ln(conv, bg_ref[...], bb_ref[...]))

    o_ref[...] = x + _bdot(jnp.concatenate([a_out, b_out], axis=1), wout_ref[...])
    tail = pbuf[tb:tb + CV_PAD, :]
    cst_ref[0] = tail
    pbuf[0:CV_PAD, :] = tail


def _even_call(xf, prev, nm, win, ag, ab, ws, bst, cw, cb, bg, bb, wout, *, nb, t_len, tb, blk0, emit_v):
    nt = t_len // tb
    whole = lambda a: pl.BlockSpec(a.shape, lambda b, t: (0,) * a.ndim)
    x_spec = pl.BlockSpec((tb, D_MODEL), lambda b, t: (blk0 + b * nt + t, 0))
    params = (nm, win, ag, ab, ws, bst, cw, cb, bg, bb, wout)
    out_shape = [jax.ShapeDtypeStruct(xf.shape, F32), jax.ShapeDtypeStruct((nb, CV_PAD, CV_WIDTH), F32)]
    out_specs = [x_spec, pl.BlockSpec((1, CV_PAD, CV_WIDTH), lambda b, t: (b, 0, 0))]
    if emit_v:
        out_shape.append(jax.ShapeDtypeStruct((nb, t_len, GM_WIDTH), F32))
        out_specs.append(pl.BlockSpec((1, tb, GM_WIDTH), lambda b, t: (b, t, 0)))
    return pl.pallas_call(
        functools.partial(_even_kernel, emit_v),
        grid=(nb, nt),
        in_specs=[x_spec, pl.BlockSpec((1, CV_PAD, CV_WIDTH), lambda b, t: (b, 0, 0))] + [whole(a) for a in params],
        out_specs=out_specs,
        out_shape=out_shape,
        scratch_shapes=[pltpu.VMEM((CV_PAD + tb, CV_WIDTH), F32)],
        input_output_aliases={0: 0},
        compiler_params=pltpu.CompilerParams(
            dimension_semantics=("arbitrary", "arbitrary"), vmem_limit_bytes=VMEM_LIMIT),
        name="even_mixer",
    )(xf, prev, *params)


def _odd_kernel(layer, x_ref, s0_ref, nm_ref, lower_ref, win_ref, ng_ref, wout_ref, o_ref, sout_ref,
                st_s, q_s, k_s, v_s, b_s, o_s):
    tb = x_ref.shape[0]
    cl = min(tb, HG_CHUNK)
    nck = tb // cl
    t = pl.program_id(1)

    @pl.when(t == 0)
    def _():
        for hh in range(HG_HEADS):
            st_s[hh] = s0_ref[0, hh].T

    x = x_ref[...]
    z = _bdot(_rms(x, nm_ref[...]), win_ref[...])
    low = lower_ref[...]
    ex = jnp.exp(low - jnp.max(low, axis=0, keepdims=True))
    soft = ex / jnp.sum(ex, axis=0, keepdims=True)
    lb = jnp.sum(soft[1:layer + 1], axis=0, keepdims=True)
    zf = z[:, D_MODEL:2 * D_MODEL]
    q = _silu(z[:, :D_MODEL])
    logf = jnp.log(lb + (1.0 - lb) * jax.nn.sigmoid(zf))
    k = (1.0 - lb) * jax.nn.sigmoid(-zf)
    vv = z[:, 2 * D_MODEL:3 * D_MODEL]

    row = lax.broadcasted_iota(jnp.int32, (cl, cl), 0)
    col = lax.broadcasted_iota(jnp.int32, (cl, cl), 1)
    tril = (row >= col).astype(F32)
    for c in range(nck):
        bc = jnp.dot(tril, logf[c * cl:(c + 1) * cl], precision=HIGHEST, preferred_element_type=F32)
        for hh in range(HG_HEADS):
            b_s[hh, c * cl:(c + 1) * cl, :] = bc[:, hh * HG_DK:(hh + 1) * HG_DK]
    for hh in range(HG_HEADS):
        sl = slice(hh * HG_DK, (hh + 1) * HG_DK)
        q_s[hh] = q[:, sl]
        k_s[hh] = k[:, sl]
        v_s[hh] = vv[:, sl]

    nblk = cl // SUB
    per = HG_SUB // SUB
    nsb = cl // HG_SUB
    rowi = lax.broadcasted_iota(jnp.int32, (SUB, HG_DK), 0)
    col0 = [HG_SUB * sb * (sb - 1) // 2 for sb in range(nsb + 1)]
    n_far_q = HG_SUB * (nsb - 1)
    n_far_k = col0[nsb]
    fr = lax.broadcasted_iota(jnp.int32, (n_far_q, n_far_k), 0) // HG_SUB + 1
    fc = lax.broadcasted_iota(jnp.int32, (n_far_q, n_far_k), 1)
    own = jnp.zeros((n_far_q, n_far_k), F32)
    for sb in range(1, nsb):
        own = jnp.where((fr == sb) & (fc >= col0[sb]) & (fc < col0[sb + 1]), 1.0, own)

    def one_head(c, hh):
        rows = pl.ds(pl.multiple_of(c * cl, cl), cl)
        qh = q_s[hh, rows, :]
        kh = k_s[hh, rows, :]
        vh = v_s[hh, rows, :]
        bh = b_s[hh, rows, :]
        st = st_s[hh]
        o = lax.dot_general((qh * jnp.exp(bh)).astype(BF16), st.astype(BF16), (((1,), (1,)), ((), ())),
                            preferred_element_type=F32)
        if nsb > 1:
            qts, kts, vs = [], [], []
            for sb in range(1, nsb):
                lo = sb * HG_SUB
                ref = bh[lo - 1:lo]
                qts.append(qh[lo:lo + HG_SUB] * jnp.exp(bh[lo:lo + HG_SUB] - ref))
                kts.append(kh[0:lo] * jnp.exp(ref - bh[0:lo]))
                vs.append(vh[0:lo])
            att = lax.dot_general(jnp.concatenate(qts, axis=0).astype(BF16), jnp.concatenate(kts, axis=0).astype(BF16),
                                  (((1,), (1,)), ((), ())), preferred_element_type=F32) * own
            far = jnp.dot(att.astype(BF16), jnp.concatenate(vs, axis=0).astype(BF16), preferred_element_type=F32)
        qb = [qh[SUB * r:SUB * (r + 1)] for r in range(nblk)]
        bb = [bh[SUB * r:SUB * (r + 1)] for r in range(nblk)]
        ob = [jnp.zeros((SUB, HG_DK), F32) for r in range(nblk)]
        for s in range(cl):
            b_src = bh[s:s + 1]
            k_src = kh[s:s + 1]
            v_src = vh[s:s + 1]
            for r in range(s // SUB, (s // HG_SUB + 1) * per):
                rel = bb[r] - b_src
                if r == s // SUB:
                    rel = jnp.where(rowi >= (s % SUB), rel, -jnp.inf)
                att_s = jnp.sum(qb[r] * jnp.exp(rel) * k_src, axis=-1, keepdims=True)
                ob[r] = ob[r] + att_s * v_src
        near = jnp.concatenate(ob, axis=0) + o
        if nsb > 1:
            near = near + jnp.concatenate([jnp.zeros((HG_SUB, HG_DK), F32), far], axis=0)
        o_s[hh, rows, :] = near
        b_last = bh[cl - 1:cl]
        kt = kh * jnp.exp(b_last - bh)
        upd = lax.dot_general(vh.astype(BF16), kt.astype(BF16), (((0,), (0,)), ((), ())),
                              preferred_element_type=F32)
        st_s[hh] = st * jnp.exp(b_last) + upd

    def chunk_heads(i, carry):
        c = i // (HG_HEADS // 2)
        first = 2 * (i % (HG_HEADS // 2))
        one_head(c, first)
        one_head(c, first + 1)
        return carry

    lax.fori_loop(0, nck * (HG_HEADS // 2), chunk_heads, 0)

    parts = []
    for hh in range(HG_HEADS):
        oh = o_s[hh]
        on = oh * lax.rsqrt(jnp.mean(oh * oh, axis=-1, keepdims=True) + EPS) * ng_ref[...]
        parts.append(on * _silu(z[:, 3 * D_MODEL + hh * HG_DK:3 * D_MODEL + (hh + 1) * HG_DK]))
    o_ref[...] = x + _bdot(jnp.concatenate(parts, axis=1), wout_ref[...])

    @pl.when(t == pl.num_programs(1) - 1)
    def _():
        for hh in range(HG_HEADS):
            sout_ref[0, hh] = st_s[hh].T


def _odd_call(xf, s0, nm, lower, win, ng, wout, *, layer, nb, t_len, tb, blk0):
    nt = t_len // tb
    whole = lambda a: pl.BlockSpec(a.shape, lambda b, t: (0,) * a.ndim)
    x_spec = pl.BlockSpec((tb, D_MODEL), lambda b, t: (blk0 + b * nt + t, 0))
    s_spec = pl.BlockSpec((1, HG_HEADS, HG_DK, HG_DK), lambda b, t: (b, 0, 0, 0))
    params = (nm, lower, win, ng, wout)
    head_buf = pltpu.VMEM((HG_HEADS, tb, HG_DK), F32)
    return pl.pallas_call(
        functools.partial(_odd_kernel, layer),
        grid=(nb, nt),
        in_specs=[x_spec, s_spec] + [whole(a) for a in params],
        out_specs=[x_spec, s_spec],
        out_shape=[jax.ShapeDtypeStruct(xf.shape, F32),
                   jax.ShapeDtypeStruct((nb, HG_HEADS, HG_DK, HG_DK), F32)],
        scratch_shapes=[pltpu.VMEM((HG_HEADS, HG_DK, HG_DK), F32), head_buf, head_buf, head_buf, head_buf, head_buf],
        input_output_aliases={0: 0},
        compiler_params=pltpu.CompilerParams(
            dimension_semantics=("arbitrary", "arbitrary"), vmem_limit_bytes=VMEM_LIMIT),
        name="odd_mixer",
    )(xf, s0, *params)


def _candidate_tiles():
    tiles = [
        [(0, b) for b in range(8)],
        [(0, b) for b in range(8, 16)],
        [(1, b) for b in range(8)],
        [(2, b) for b in range(5)] + [(5, 0), (5, 1), (8, 0)],
        [(3, b) for b in range(4)] + [(4, b) for b in range(3)] + [(9, 0)],
        [(6, 0), (6, 1), (7, 0), (7, 1), (10, 0), (11, 0), (12, 0), (13, 0)],
        [(14, 0), (15, 0), (2, 5), (2, 6), (2, 7), (3, 4), (3, 5), (3, 6)],
    ]
    flat = [ab for tile in tiles for ab in tile]
    need = [(a, b) for a in range(PK_TOPK) for b in range(PK_TOPK) if (a + 1) * (b + 1) <= PK_TOPK]
    assert len(set(flat)) == len(flat) and set(need) <= set(flat) and all(len(t) == SUB for t in tiles)
    return tiles


def _peer_select_kernel(x_ref, nf_ref, wq_ref, keys_ref, eidx_ref, gate_ref, st_s, tv_s, ti_s, es_s, gs_s):
    tb = x_ref.shape[0]
    nlt = tb // LANE
    q = _bdot(_rms(x_ref[...], nf_ref[...]), wq_ref[...])
    for hp in range(2 * PK_HEADS):
        sc = lax.dot_general(keys_ref[hp], q[:, hp * PK_HALF:(hp + 1) * PK_HALF].astype(BF16),
                             (((1,), (1,)), ((), ())), preferred_element_type=F32)
        for lt in range(nlt):
            st_s[hp, lt] = sc[:, lt * LANE:(lt + 1) * LANE]

    kiota = lax.broadcasted_iota(jnp.int32, (PK_NKEYS, LANE), 0)
    rowi = lax.broadcasted_iota(jnp.int32, (SUB, LANE), 0)
    tiles = _candidate_tiles()
    pos_tiles = []
    for tile in tiles:
        p = jnp.full((SUB, LANE), tile[0][0] * PK_TOPK + tile[0][1], jnp.int32)
        for r in range(1, SUB):
            p = jnp.where(rowi >= r, tile[r][0] * PK_TOPK + tile[r][1], p)
        pos_tiles.append(p)
    pos = jnp.concatenate(pos_tiles, axis=0)
    big = PK_TOPK * PK_TOPK

    def pair_rows(tile, first, second):
        fa = sb = None
        rolled = {}
        for r, (a, b) in enumerate(tile):
            if r == 0 or a != tile[r - 1][0]:
                row = jnp.broadcast_to(first[a:a + 1], (SUB, LANE))
                fa = row if r == 0 else jnp.where(rowi >= r, row, fa)
            key = (b // SUB, (r - b) % SUB)
            if r == 0 or key != prev_key:
                if key not in rolled:
                    src = second[key[0] * SUB:(key[0] + 1) * SUB]
                    rolled[key] = src if key[1] == 0 else pltpu.roll(src, key[1], 0)
                sb = rolled[key] if r == 0 else jnp.where(rowi >= r, rolled[key], sb)
            prev_key = key
        return fa, sb

    streams1 = [(p, lt) for lt in range(nlt) for p in range(2)]

    def head(hd, carry):
        ss = [st_s[2 * hd + p, lt] for p, lt in streams1]
        for r in range(PK_TOPK):
            for n, (p, lt) in enumerate(streams1):
                s = ss[n]
                m = jnp.max(s, axis=0, keepdims=True)
                ix = jnp.min(jnp.where(s == m, kiota, PK_NKEYS), axis=0, keepdims=True)
                tv_s[2 * hd + p, lt, r:r + 1, :] = m
                ti_s[2 * hd + p, lt, r:r + 1, :] = ix
                ss[n] = jnp.where(kiota == ix, -jnp.inf, s)
        cands, cidxs, bests = [], [], [[] for _ in range(nlt)]
        for lt in range(nlt):
            s1 = tv_s[2 * hd, lt]
            i1 = ti_s[2 * hd, lt]
            s2 = tv_s[2 * hd + 1, lt]
            i2 = ti_s[2 * hd + 1, lt]
            sums, ids = [], []
            for tile in tiles:
                va, vb = pair_rows(tile, s1, s2)
                ia, ib = pair_rows(tile, i1, i2)
                sums.append(va + vb)
                ids.append(ia * PK_NKEYS + ib)
            cands.append(jnp.concatenate(sums, axis=0))
            cidxs.append(jnp.concatenate(ids, axis=0))
        for r in range(PK_TOPK):
            for lt in range(nlt):
                cand = cands[lt]
                m = jnp.max(cand, axis=0, keepdims=True)
                pick = jnp.min(jnp.where(cand == m, pos, big), axis=0, keepdims=True)
                sel = pos == pick
                es_s[lt, hd, r:r + 1, :] = jnp.max(jnp.where(sel, cidxs[lt], -1), axis=0,
                                                   keepdims=True).astype(F32)
                bests[lt].append(m)
                cands[lt] = jnp.where(sel, -jnp.inf, cand)
        for lt in range(nlt):
            ex = [jnp.exp(b - bests[lt][0]) for b in bests[lt]]
            den = ex[0]
            for r in range(1, PK_TOPK):
                den = den + ex[r]
            for r in range(PK_TOPK):
                gs_s[lt, hd, r:r + 1, :] = ex[r] / den
        return carry

    lax.fori_loop(0, PK_HEADS, head, 0)
    for lt in range(nlt):
        rows = slice(lt * LANE, (lt + 1) * LANE)
        eidx_ref[:, rows] = es_s[lt].reshape(PK_PAIRS, LANE).astype(jnp.int32)
        gate_ref[rows, :] = gs_s[lt].reshape(PK_PAIRS, LANE).T


def _peer_select(xf, nf, wq, keys):
    n_tok = xf.shape[0]
    nlt = SEL_TB // LANE
    whole = lambda a: pl.BlockSpec(a.shape, lambda i: (0,) * a.ndim)
    return pl.pallas_call(
        _peer_select_kernel,
        grid=(n_tok // SEL_TB,),
        in_specs=[pl.BlockSpec((SEL_TB, D_MODEL), lambda i: (i, 0)), whole(nf), whole(wq), whole(keys)],
        out_specs=[pl.BlockSpec((PK_PAIRS, SEL_TB), lambda i: (0, i)),
                   pl.BlockSpec((SEL_TB, PK_PAIRS), lambda i: (i, 0))],
        out_shape=[jax.ShapeDtypeStruct((PK_PAIRS, n_tok), jnp.int32),
                   jax.ShapeDtypeStruct((n_tok, PK_PAIRS), F32)],
        scratch_shapes=[pltpu.VMEM((2 * PK_HEADS, nlt, PK_NKEYS, LANE), F32),
                        pltpu.VMEM((2 * PK_HEADS, nlt, PK_TOPK, LANE), F32),
                        pltpu.VMEM((2 * PK_HEADS, nlt, PK_TOPK, LANE), jnp.int32),
                        pltpu.VMEM((nlt, PK_HEADS, PK_TOPK, LANE), F32),
                        pltpu.VMEM((nlt, PK_HEADS, PK_TOPK, LANE), F32)],
        compiler_params=pltpu.CompilerParams(
            dimension_semantics=("arbitrary",), vmem_limit_bytes=VMEM_LIMIT),
        name="peer_select",
    )(xf, nf, wq, keys)


def _pack_kernel(t_ref, o_ref):
    bits = lax.bitcast_convert_type(t_ref[...].astype(BF16).astype(F32), jnp.uint32)
    word = (bits[:, :HALF] >> 16) | bits[:, HALF:]
    for s in range(HSUB):
        o_ref[:, s, :] = word[:, s * LANE:(s + 1) * LANE]


def _pack_table(tab):
    return pl.pallas_call(
        _pack_kernel,
        grid=(PK_EXPERTS // PACK_ROWS,),
        in_specs=[pl.BlockSpec((PACK_ROWS, D_MODEL), lambda i: (i, 0))],
        out_specs=pl.BlockSpec((PACK_ROWS, HSUB, LANE), lambda i: (i, 0, 0)),
        out_shape=jax.ShapeDtypeStruct((PK_EXPERTS, HSUB, LANE), jnp.uint32),
        compiler_params=pltpu.CompilerParams(dimension_semantics=("arbitrary",)),
        name="pack_table",
    )(tab)


N_PIECE = 2


def _split(x):
    p1 = x.astype(BF16).astype(F32)
    p2 = (x - p1).astype(BF16).astype(F32)
    return p1, p2


def _index_copies(idx_hbm, idx_smem, sems):
    first = pl.program_id(0) * PEER_TB
    return [pltpu.make_async_copy(idx_hbm.at[k, pl.ds(first, PEER_TB)], idx_smem[k], sems.at[k])
            for k in range(PK_PAIRS)]


def _gather_rows(tab_ref, idx_smem, t, stage):
    for k in range(PK_PAIRS):
        stage[pl.ds(k, HSUB, stride=STAGE_STRIDE), :] = tab_ref[idx_smem[k][t]]


def _stage_block(stage, s):
    return pltpu.bitcast(stage[s * STAGE_STRIDE:s * STAGE_STRIDE + PK_PAIRS, :], BF16)


PIPE_TOK = 2


def _token_pipeline(tab_ref, idx_smem, stages, compute):
    sets = (stages[:PIPE_TOK], stages[PIPE_TOK:])
    for j in range(PIPE_TOK):
        _gather_rows(tab_ref, idx_smem, j, sets[0][j])

    def step(t, cur, nxt):
        for j in range(PIPE_TOK):
            _gather_rows(tab_ref, idx_smem, jnp.minimum(t + PIPE_TOK + j, PEER_TB - 1), nxt[j])
        for j in range(PIPE_TOK):
            compute(t + j, cur[j])

    def body(i, carry):
        @pl.when(i % 2 == 0)
        def _():
            step(PIPE_TOK * i, sets[0], sets[1])

        @pl.when(i % 2 == 1)
        def _():
            step(PIPE_TOK * i, sets[1], sets[0])

        return carry

    lax.fori_loop(0, PEER_TB // PIPE_TOK, body, 0)


def _peer_u_kernel(idx_hbm, x_ref, nf_ref, gate_ref, tab_ref, w_ref, sems, hs, ybuf, *bufs):
    stages, idx_smem = bufs[:2 * PIPE_TOK], bufs[2 * PIPE_TOK:]
    copies = _index_copies(idx_hbm, idx_smem, sems)
    for cp in copies:
        cp.start()
    pieces = _split(_rms(x_ref[...], nf_ref[...]))
    hs[:, :, 2 * N_PIECE:, :] = jnp.zeros((PEER_TB, HSUB, SUB - 2 * N_PIECE, LANE), F32)
    for i in range(N_PIECE):
        for s in range(HSUB):
            hs[:, s, i, :] = pieces[i][:, s * LANE:(s + 1) * LANE]
            hs[:, s, N_PIECE + i, :] = pieces[i][:, HALF + s * LANE:HALF + (s + 1) * LANE]
    for cp in copies:
        cp.wait()
    rowi = lax.broadcasted_iota(jnp.int32, (SUB, 2 * PK_PAIRS), 0)
    odd = lax.broadcasted_iota(jnp.int32, (SUB, 2 * PK_PAIRS), 1) % 2
    keep = jnp.where(odd == 0, (rowi < N_PIECE).astype(F32),
                     ((rowi >= N_PIECE) & (rowi < 2 * N_PIECE)).astype(F32))

    def compute(t, stage):
        acc = jnp.zeros((SUB, 2 * PK_PAIRS), F32)
        for s in range(HSUB):
            acc = acc + lax.dot_general(hs[t, s].astype(BF16), _stage_block(stage, s), (((1,), (1,)), ((), ())),
                                        preferred_element_type=F32)
        ybuf[pl.ds(t, 1), :] = jnp.sum(acc * keep, axis=0, keepdims=True)

    _token_pipeline(tab_ref, idx_smem, stages, compute)
    r = lax.broadcasted_iota(jnp.int32, (2 * PK_PAIRS, PK_PAIRS), 0)
    c = lax.broadcasted_iota(jnp.int32, (2 * PK_PAIRS, PK_PAIRS), 1)
    fold = (r // 2 == c).astype(F32)
    act = jnp.dot(ybuf[...], fold, precision=HIGHEST, preferred_element_type=F32)
    w_ref[...] = gate_ref[...] * _gelu_tanh(act)


def _peer_u(eidx, xf, nf, gate, tab_packed):
    n_tok = xf.shape[0]
    stages = [pltpu.VMEM((HSUB * STAGE_STRIDE, LANE), jnp.uint32)] * (2 * PIPE_TOK)
    return pl.pallas_call(
        _peer_u_kernel,
        grid=(n_tok // PEER_TB,),
        in_specs=[
            pl.BlockSpec(memory_space=pl.ANY),
            pl.BlockSpec((PEER_TB, D_MODEL), lambda i: (i, 0)),
            pl.BlockSpec(nf.shape, lambda i: (0, 0)),
            pl.BlockSpec((PEER_TB, PK_PAIRS), lambda i: (i, 0)),
            pl.BlockSpec(memory_space=pltpu.VMEM),
        ],
        out_specs=pl.BlockSpec((PEER_TB, PK_PAIRS), lambda i: (i, 0)),
        out_shape=jax.ShapeDtypeStruct((n_tok, PK_PAIRS), F32),
        scratch_shapes=[
            pltpu.SemaphoreType.DMA((PK_PAIRS,)),
            pltpu.VMEM((PEER_TB, HSUB, SUB, LANE), F32),
            pltpu.VMEM((PEER_TB, 2 * PK_PAIRS), F32),
        ] + stages + [pltpu.SMEM((PEER_TB,), jnp.int32)] * PK_PAIRS,
        compiler_params=pltpu.CompilerParams(
            dimension_semantics=("arbitrary",), vmem_limit_bytes=VMEM_LIMIT),
        name="peer_u",
    )(eidx, xf, nf, gate, tab_packed)


def _peer_v_kernel(idx_hbm, w_ref, x_ref, tab_ref, o_ref, sems, av, ov, *bufs):
    stages, idx_smem = bufs[:2 * PIPE_TOK], bufs[2 * PIPE_TOK:]
    copies = _index_copies(idx_hbm, idx_smem, sems)
    for cp in copies:
        cp.start()
    r = lax.broadcasted_iota(jnp.int32, (PK_PAIRS, 2 * PK_PAIRS), 0)
    c = lax.broadcasted_iota(jnp.int32, (PK_PAIRS, 2 * PK_PAIRS), 1)
    spread = [(c == 2 * r).astype(BF16), (c == 2 * r + 1).astype(BF16)]
    pieces = _split(w_ref[...])
    av[:, 2 * N_PIECE:, :] = jnp.zeros((PEER_TB, SUB - 2 * N_PIECE, 2 * PK_PAIRS), F32)
    for i in range(N_PIECE):
        for j in range(2):
            av[:, N_PIECE * j + i, :] = jnp.dot(pieces[i].astype(BF16), spread[j], preferred_element_type=F32)
    for cp in copies:
        cp.wait()

    def compute(t, stage):
        wt = av[t].astype(BF16)
        lo, hi = [], []
        for s in range(HSUB):
            y = jnp.dot(wt, _stage_block(stage, s), preferred_element_type=F32)
            lo.append(y[0:1] + y[1:2])
            hi.append(y[2:3] + y[3:4])
        ov[pl.ds(t, 1), :] = jnp.concatenate(lo + hi, axis=1)

    _token_pipeline(tab_ref, idx_smem, stages, compute)
    o_ref[...] = x_ref[...] + ov[...]


def _peer_v(eidx, w, xf, tab_packed):
    n_tok = xf.shape[0]
    stages = [pltpu.VMEM((HSUB * STAGE_STRIDE, LANE), jnp.uint32)] * (2 * PIPE_TOK)
    return pl.pallas_call(
        _peer_v_kernel,
        grid=(n_tok // PEER_TB,),
        in_specs=[
            pl.BlockSpec(memory_space=pl.ANY),
            pl.BlockSpec((PEER_TB, PK_PAIRS), lambda i: (i, 0)),
            pl.BlockSpec((PEER_TB, D_MODEL), lambda i: (i, 0)),
            pl.BlockSpec(memory_space=pltpu.VMEM),
        ],
        out_specs=pl.BlockSpec((PEER_TB, D_MODEL), lambda i: (i, 0)),
        out_shape=jax.ShapeDtypeStruct((n_tok, D_MODEL), F32),
        scratch_shapes=[
            pltpu.SemaphoreType.DMA((PK_PAIRS,)),
            pltpu.VMEM((PEER_TB, SUB, 2 * PK_PAIRS), F32),
            pltpu.VMEM((PEER_TB, D_MODEL), F32),
        ] + stages + [pltpu.SMEM((PEER_TB,), jnp.int32)] * PK_PAIRS,
        compiler_params=pltpu.CompilerParams(
            dimension_semantics=("arbitrary",), vmem_limit_bytes=VMEM_LIMIT),
        name="peer_v",
    )(eidx, w, xf, tab_packed)


def _final_kernel(x_ref, g_ref, o_ref):
    o_ref[...] = _rms(x_ref[...], g_ref[...])


def _final_call(xf, g, *, n_rows, blk0):
    nblk = n_rows // MIX_TB
    return pl.pallas_call(
        _final_kernel,
        grid=(nblk,),
        in_specs=[pl.BlockSpec((MIX_TB, D_MODEL), lambda i: (blk0 + i, 0)), pl.BlockSpec(g.shape, lambda i: (0, 0))],
        out_specs=pl.BlockSpec((MIX_TB, D_MODEL), lambda i: (i, 0)),
        out_shape=jax.ShapeDtypeStruct((n_rows, D_MODEL), F32),
        compiler_params=pltpu.CompilerParams(dimension_semantics=("arbitrary",)),
        name="final_norm",
    )(xf, g)


def kernel(x_prompt, x_sample, state_conv, state_hgrn, norm_mix, norm_ffn, norm_final, ev_w_in, ev_a_ln_g, ev_a_ln_b, ev_ws, ev_bs, ev_conv_w, ev_conv_b, ev_b_ln_g, ev_b_ln_b, ev_w_out, od_w_in, od_lower, od_norm_g, od_w_out, peer_wq, peer_keys, peer_u, peer_v):
    row = lambda a: a.reshape(1, -1)
    xf = jnp.concatenate([x_prompt.reshape(N_PROMPT, D_MODEL), x_sample.reshape(N_SAMPLE, D_MODEL)], axis=0)
    pad_hist = lambda a: jnp.pad(a, ((0, 0), (CV_PAD - CV_STATE, 0), (0, 0)))
    conv_p, conv_s, hg_p, hg_s, v_s = [], [], [], [], []
    prompt = dict(nb=BATCH, t_len=SEQ, tb=MIX_TB, blk0=0)
    sample = dict(nb=DEC_BATCH, t_len=DEC_SEQ, tb=DEC_SEQ, blk0=N_PROMPT // DEC_SEQ)
    for l in range(DEPTH):
        j = l // 2
        if l % 2 == 0:
            prm = (row(norm_mix[l]), ev_w_in[j].astype(BF16), row(ev_a_ln_g[j]), row(ev_a_ln_b[j]), ev_ws[j],
                   ev_bs[j].T, ev_conv_w[j], row(ev_conv_b[j]), row(ev_b_ln_g[j]), row(ev_b_ln_b[j]),
                   ev_w_out[j].astype(BF16))
            zeros = jnp.zeros((BATCH, CV_PAD, CV_WIDTH), F32)
            xf, cp = _even_call(xf, zeros, *prm, emit_v=False, **prompt)
            xf, cs, vs = _even_call(xf, pad_hist(state_conv[j]), *prm, emit_v=True, **sample)
            conv_p.append(cp[:, CV_PAD - CV_STATE:])
            conv_s.append(cs[:, CV_PAD - CV_STATE:])
            v_s.append(vs)
        else:
            prm = (row(norm_mix[l]), od_lower, od_w_in[j].astype(BF16), row(od_norm_g[j]), od_w_out[j].astype(BF16))
            zeros = jnp.zeros((BATCH, HG_HEADS, HG_DK, HG_DK), F32)
            xf, sp = _odd_call(xf, zeros, *prm, layer=l, **prompt)
            xf, ss = _odd_call(xf, state_hgrn[j], *prm, layer=l, **sample)
            hg_p.append(sp)
            hg_s.append(ss)
        nf = row(norm_ffn[l])
        keys = peer_keys[l].reshape(2 * PK_HEADS, PK_NKEYS, PK_HALF).astype(BF16)
        eidx, gate = _peer_select(xf, nf, peer_wq[l].astype(BF16), keys)
        w = _peer_u(eidx, xf, nf, gate, _pack_table(peer_u[l]))
        xf = _peer_v(eidx, w, xf, _pack_table(peer_v[l]))
    g = row(norm_final)
    y_prompt = _final_call(xf, g, n_rows=N_PROMPT, blk0=0).reshape(BATCH, SEQ, D_MODEL)
    y_sample = _final_call(xf, g, n_rows=N_SAMPLE, blk0=N_PROMPT // MIX_TB).reshape(DEC_BATCH, DEC_SEQ, D_MODEL)
    return (y_prompt, y_sample, jnp.stack(conv_p), jnp.stack(conv_s), jnp.stack(hg_p), jnp.stack(hg_s),
            jnp.stack(v_s))
```

```python
import functools

import jax
import jax.numpy as jnp
from jax import lax
from jax.experimental import pallas as pl
from jax.experimental.pallas import tpu as pltpu

F32 = jnp.float32
BF16 = jnp.bfloat16
HIGHEST = lax.Precision.HIGHEST

SUB = 8
LANE = 128

D_MODEL = 1024
BATCH = 2
SEQ = 16384
DEPTH = 4
DEC_BATCH = 8
DEC_SEQ = 32
N_PROMPT = BATCH * SEQ
N_SAMPLE = DEC_BATCH * DEC_SEQ
N_TOK = N_PROMPT + N_SAMPLE
EPS = 1e-6

GM_WIDTH = D_MODEL // 2
GM_GROUPS = 4
GM_GROUP_DIM = GM_WIDTH // GM_GROUPS
GM_CHUNK = 128
CV_WIDTH = D_MODEL // 2
CV_KERNEL = 31
CV_STATE = CV_KERNEL - 1
CV_PAD = 32
HG_HEADS = 8
HG_DK = D_MODEL // HG_HEADS
HG_CHUNK = 64
HG_SUB = 16
PK_HEADS = 8
PK_NKEYS = 128
PK_EXPERTS = PK_NKEYS * PK_NKEYS
PK_HALF = 128
PK_TOPK = 16
PK_PAIRS = PK_HEADS * PK_TOPK
HALF = D_MODEL // 2
HSUB = HALF // LANE
STAGE_STRIDE = 136

MIX_TB = 256
SEL_TB = 256
PEER_TB = 256
PACK_ROWS = 512
VMEM_LIMIT = 48 * 1024 * 1024


def _gelu_tanh(x):
    return 0.5 * x * (1.0 + jnp.tanh(0.7978845608028654 * (x + 0.044715 * x * x * x)))


def _silu(x):
    return x * jax.nn.sigmoid(x)


def _rms(x, g):
    return x * lax.rsqrt(jnp.mean(x * x, axis=-1, keepdims=True) + EPS) * g


def _ln(x, g, b):
    xc = x - jnp.mean(x, axis=-1, keepdims=True)
    return xc * lax.rsqrt(jnp.mean(xc * xc, axis=-1, keepdims=True) + EPS) * g + b


def _bdot(a, b):
    return jnp.dot(a.astype(BF16), b.astype(BF16), preferred_element_type=F32)


def _even_kernel(emit_v, x_ref, prev_ref, nm_ref, win_ref, ag_ref, ab_ref, ws_ref, bst_ref, cw_ref, cb_ref,
                 bg_ref, bb_ref, wout_ref, o_ref, cst_ref, *rest):
    if emit_v:
        v_ref, pbuf = rest
    else:
        (pbuf,) = rest
    tb = x_ref.shape[0]
    gl = min(tb, GM_CHUNK)

    @pl.when(pl.program_id(1) == 0)
    def _():
        pbuf[0:CV_PAD, :] = prev_ref[0]

    x = x_ref[...]
    z = _bdot(_rms(x, nm_ref[...]), win_ref[...])
    za = _gelu_tanh(z[:, :2 * GM_WIDTH])
    u = za[:, :GM_WIDTH]
    v = _ln(za[:, GM_WIDTH:], ag_ref[...], ab_ref[...])
    if emit_v:
        v_ref[0] = v

    row = lax.broadcasted_iota(jnp.int32, (gl, gl), 0)
    col = lax.broadcasted_iota(jnp.int32, (gl, gl), 1)
    wts = [jnp.where(row >= col, ws_ref[g, 0:gl, 0:gl], 0.0) for g in range(GM_GROUPS)]
    chunks = []
    for c in range(tb // gl):
        vc = v[c * gl:(c + 1) * gl]
        parts = [jnp.dot(wts[g], vc[:, g * GM_GROUP_DIM:(g + 1) * GM_GROUP_DIM], precision=HIGHEST,
                         preferred_element_type=F32) + bst_ref[0:gl, g:g + 1] for g in range(GM_GROUPS)]
        chunks.append(jnp.concatenate(parts, axis=1))
    a_out = u * jnp.concatenate(chunks, axis=0)

    zb = z[:, 2 * GM_WIDTH:]
    pbuf[CV_PAD:CV_PAD + tb, :] = zb[:, :CV_WIDTH] * jax.nn.sigmoid(zb[:, CV_WIDTH:])
    first = CV_PAD - CV_STATE
    conv = cb_ref[...] + cw_ref[0:1, :] * pbuf[first:first + tb, :]
    for j in range(1, CV_KERNEL):
        conv = conv + cw_ref[j:j + 1, :] * pbuf[first + j:first + j + tb, :]
    b_out = _silu(_ln(conv, bg_ref[...], bb_ref[...]))

    o_ref[...] = x + _bdot(jnp.concatenate([a_out, b_out], axis=1), wout_ref[...])
    tail = pbuf[tb:tb + CV_PAD, :]
    cst_ref[0] = tail
    pbuf[0:CV_PAD, :] = tail


def _even_call(xf, prev, nm, win, ag, ab, ws, bst, cw, cb, bg, bb, wout, *, nb, t_len, tb, blk0, emit_v):
    nt = t_len // tb
    whole = lambda a: pl.BlockSpec(a.shape, lambda b, t: (0,) * a.ndim)
    x_spec = pl.BlockSpec((tb, D_MODEL), lambda b, t: (blk0 + b * nt + t, 0))
    params = (nm, win, ag, ab, ws, bst, cw, cb, bg, bb, wout)
    out_shape = [jax.ShapeDtypeStruct(xf.shape, F32), jax.ShapeDtypeStruct((nb, CV_PAD, CV_WIDTH), F32)]
    out_specs = [x_spec, pl.BlockSpec((1, CV_PAD, CV_WIDTH), lambda b, t: (b, 0, 0))]
    if emit_v:
        out_shape.append(jax.ShapeDtypeStruct((nb, t_len, GM_WIDTH), F32))
        out_specs.append(pl.BlockSpec((1, tb, GM_WIDTH), lambda b, t: (b, t, 0)))
    return pl.pallas_call(
        functools.partial(_even_kernel, emit_v),
        grid=(nb, nt),
        in_specs=[x_spec, pl.BlockSpec((1, CV_PAD, CV_WIDTH), lambda b, t: (b, 0, 0))] + [whole(a) for a in params],
        out_specs=out_specs,
        out_shape=out_shape,
        scratch_shapes=[pltpu.VMEM((CV_PAD + tb, CV_WIDTH), F32)],
        input_output_aliases={0: 0},
        compiler_params=pltpu.CompilerParams(
            dimension_semantics=("arbitrary", "arbitrary"), vmem_limit_bytes=VMEM_LIMIT),
        name="even_mixer",
    )(xf, prev, *params)


def _odd_kernel(layer, x_ref, s0_ref, nm_ref, lower_ref, win_ref, ng_ref, wout_ref, o_ref, sout_ref,
                st_s, q_s, k_s, v_s, b_s, o_s):
    tb = x_ref.shape[0]
    cl = min(tb, HG_CHUNK)
    nck = tb // cl
    t = pl.program_id(1)

    @pl.when(t == 0)
    def _():
        for hh in range(HG_HEADS):
            st_s[hh] = s0_ref[0, hh].T

    x = x_ref[...]
    z = _bdot(_rms(x, nm_ref[...]), win_ref[...])
    low = lower_ref[...]
    ex = jnp.exp(low - jnp.max(low, axis=0, keepdims=True))
    soft = ex / jnp.sum(ex, axis=0, keepdims=True)
    lb = jnp.sum(soft[1:layer + 1], axis=0, keepdims=True)
    zf = z[:, D_MODEL:2 * D_MODEL]
    q = _silu(z[:, :D_MODEL])
    logf = jnp.log(lb + (1.0 - lb) * jax.nn.sigmoid(zf))
    k = (1.0 - lb) * jax.nn.sigmoid(-zf)
    vv = z[:, 2 * D_MODEL:3 * D_MODEL]

    row = lax.broadcasted_iota(jnp.int32, (cl, cl), 0)
    col = lax.broadcasted_iota(jnp.int32, (cl, cl), 1)
    tril = (row >= col).astype(F32)
    for c in range(nck):
        bc = jnp.dot(tril, logf[c * cl:(c + 1) * cl], precision=HIGHEST, preferred_element_type=F32)
        for hh in range(HG_HEADS):
            b_s[hh, c * cl:(c + 1) * cl, :] = bc[:, hh * HG_DK:(hh + 1) * HG_DK]
    for hh in range(HG_HEADS):
        sl = slice(hh * HG_DK, (hh + 1) * HG_DK)
        q_s[hh] = q[:, sl]
        k_s[hh] = k[:, sl]
        v_s[hh] = vv[:, sl]

    nblk = cl // SUB
    per = HG_SUB // SUB
    nsb = cl // HG_SUB
    rowi = lax.broadcasted_iota(jnp.int32, (SUB, HG_DK), 0)
    col0 = [HG_SUB * sb * (sb - 1) // 2 for sb in range(nsb + 1)]
    n_far_q = HG_SUB * (nsb - 1)
    n_far_k = col0[nsb]
    fr = lax.broadcasted_iota(jnp.int32, (n_far_q, n_far_k), 0) // HG_SUB + 1
    fc = lax.broadcasted_iota(jnp.int32, (n_far_q, n_far_k), 1)
    own = jnp.zeros((n_far_q, n_far_k), F32)
    for sb in range(1, nsb):
        own = jnp.where((fr == sb) & (fc >= col0[sb]) & (fc < col0[sb + 1]), 1.0, own)

    def one_head(c, hh):
        rows = pl.ds(pl.multiple_of(c * cl, cl), cl)
        qh = q_s[hh, rows, :]
        kh = k_s[hh, rows, :]
        vh = v_s[hh, rows, :]
        bh = b_s[hh, rows, :]
        st = st_s[hh]
        o = lax.dot_general((qh * jnp.exp(bh)).astype(BF16), st.astype(BF16), (((1,), (1,)), ((), ())),
                            preferred_element_type=F32)
        if nsb > 1:
            qts, kts, vs = [], [], []
            for sb in range(1, nsb):
                lo = sb * HG_SUB
                ref = bh[lo - 1:lo]
                qts.append(qh[lo:lo + HG_SUB] * jnp.exp(bh[lo:lo + HG_SUB] - ref))
                kts.append(kh[0:lo] * jnp.exp(ref - bh[0:lo]))
                vs.append(vh[0:lo])
            att = lax.dot_general(jnp.concatenate(qts, axis=0).astype(BF16), jnp.concatenate(kts, axis=0).astype(BF16),
                                  (((1,), (1,)), ((), ())), preferred_element_type=F32) * own
            far = jnp.dot(att.astype(BF16), jnp.concatenate(vs, axis=0).astype(BF16), preferred_element_type=F32)
        qb = [qh[SUB * r:SUB * (r + 1)] for r in range(nblk)]
        bb = [bh[SUB * r:SUB * (r + 1)] for r in range(nblk)]
        ob = [jnp.zeros((SUB, HG_DK), F32) for r in range(nblk)]
        for s in range(cl):
            b_src = bh[s:s + 1]
            k_src = kh[s:s + 1]
            v_src = vh[s:s + 1]
            for r in range(s // SUB, (s // HG_SUB + 1) * per):
                rel = bb[r] - b_src
                if r == s // SUB:
                    rel = jnp.where(rowi >= (s % SUB), rel, -jnp.inf)
                att_s = jnp.sum(qb[r] * jnp.exp(rel) * k_src, axis=-1, keepdims=True)
                ob[r] = ob[r] + att_s * v_src
        near = jnp.concatenate(ob, axis=0) + o
        if nsb > 1:
            near = near + jnp.concatenate([jnp.zeros((HG_SUB, HG_DK), F32), far], axis=0)
        o_s[hh, rows, :] = near
        b_last = bh[cl - 1:cl]
        kt = kh * jnp.exp(b_last - bh)
        upd = lax.dot_general(vh.astype(BF16), kt.astype(BF16), (((0,), (0,)), ((), ())),
                              preferred_element_type=F32)
        st_s[hh] = st * jnp.exp(b_last) + upd

    def chunk_heads(i, carry):
        c = i // (HG_HEADS // 2)
        first = 2 * (i % (HG_HEADS // 2))
        one_head(c, first)
        one_head(c, first + 1)
        return carry

    lax.fori_loop(0, nck * (HG_HEADS // 2), chunk_heads, 0)

    parts = []
    for hh in range(HG_HEADS):
        oh = o_s[hh]
        on = oh * lax.rsqrt(jnp.mean(oh * oh, axis=-1, keepdims=True) + EPS) * ng_ref[...]
        parts.append(on * _silu(z[:, 3 * D_MODEL + hh * HG_DK:3 * D_MODEL + (hh + 1) * HG_DK]))
    o_ref[...] = x + _bdot(jnp.concatenate(parts, axis=1), wout_ref[...])

    @pl.when(t == pl.num_programs(1) - 1)
    def _():
        for hh in range(HG_HEADS):
            sout_ref[0, hh] = st_s[hh].T


def _odd_call(xf, s0, nm, lower, win, ng, wout, *, layer, nb, t_len, tb, blk0):
    nt = t_len // tb
    whole = lambda a: pl.BlockSpec(a.shape, lambda b, t: (0,) * a.ndim)
    x_spec = pl.BlockSpec((tb, D_MODEL), lambda b, t: (blk0 + b * nt + t, 0))
    s_spec = pl.BlockSpec((1, HG_HEADS, HG_DK, HG_DK), lambda b, t: (b, 0, 0, 0))
    params = (nm, lower, win, ng, wout)
    head_buf = pltpu.VMEM((HG_HEADS, tb, HG_DK), F32)
    return pl.pallas_call(
        functools.partial(_odd_kernel, layer),
        grid=(nb, nt),
        in_specs=[x_spec, s_spec] + [whole(a) for a in params],
        out_specs=[x_spec, s_spec],
        out_shape=[jax.ShapeDtypeStruct(xf.shape, F32),
                   jax.ShapeDtypeStruct((nb, HG_HEADS, HG_DK, HG_DK), F32)],
        scratch_shapes=[pltpu.VMEM((HG_HEADS, HG_DK, HG_DK), F32), head_buf, head_buf, head_buf, head_buf, head_buf],
        input_output_aliases={0: 0},
        compiler_params=pltpu.CompilerParams(
            dimension_semantics=("arbitrary", "arbitrary"), vmem_limit_bytes=VMEM_LIMIT),
        name="odd_mixer",
    )(xf, s0, *params)


def _candidate_tiles():
    tiles = [
        [(0, b) for b in range(8)],
        [(0, b) for b in range(8, 16)],
        [(1, b) for b in range(8)],
        [(2, b) for b in range(5)] + [(5, 0), (5, 1), (8, 0)],
        [(3, b) for b in range(4)] + [(4, b) for b in range(3)] + [(9, 0)],
        [(6, 0), (6, 1), (7, 0), (7, 1), (10, 0), (11, 0), (12, 0), (13, 0)],
        [(14, 0), (15, 0), (2, 5), (2, 6), (2, 7), (3, 4), (3, 5), (3, 6)],
    ]
    flat = [ab for tile in tiles for ab in tile]
    need = [(a, b) for a in range(PK_TOPK) for b in range(PK_TOPK) if (a + 1) * (b + 1) <= PK_TOPK]
    assert len(set(flat)) == len(flat) and set(need) <= set(flat) and all(len(t) == SUB for t in tiles)
    return tiles


def _peer_select_kernel(x_ref, nf_ref, wq_ref, keys_ref, eidx_ref, gate_ref, st_s, tv_s, ti_s, es_s, gs_s):
    tb = x_ref.shape[0]
    nlt = tb // LANE
    q = _bdot(_rms(x_ref[...], nf_ref[...]), wq_ref[...])
    for hp in range(2 * PK_HEADS):
        sc = lax.dot_general(keys_ref[hp], q[:, hp * PK_HALF:(hp + 1) * PK_HALF].astype(BF16),
                             (((1,), (1,)), ((), ())), preferred_element_type=F32)
        for lt in range(nlt):
            st_s[hp, lt] = sc[:, lt * LANE:(lt + 1) * LANE]

    kiota = lax.broadcasted_iota(jnp.int32, (PK_NKEYS, LANE), 0)
    rowi = lax.broadcasted_iota(jnp.int32, (SUB, LANE), 0)
    tiles = _candidate_tiles()
    pos_tiles = []
    for tile in tiles:
        p = jnp.full((SUB, LANE), tile[0][0] * PK_TOPK + tile[0][1], jnp.int32)
        for r in range(1, SUB):
            p = jnp.where(rowi >= r, tile[r][0] * PK_TOPK + tile[r][1], p)
        pos_tiles.append(p)
    pos = jnp.concatenate(pos_tiles, axis=0)
    big = PK_TOPK * PK_TOPK

    def pair_rows(tile, first, second):
        fa = sb = None
        rolled = {}
        for r, (a, b) in enumerate(tile):
            if r == 0 or a != tile[r - 1][0]:
                row = jnp.broadcast_to(first[a:a + 1], (SUB, LANE))
                fa = row if r == 0 else jnp.where(rowi >= r, row, fa)
            key = (b // SUB, (r - b) % SUB)
            if r == 0 or key != prev_key:
                if key not in rolled:
                    src = second[key[0] * SUB:(key[0] + 1) * SUB]
                    rolled[key] = src if key[1] == 0 else pltpu.roll(src, key[1], 0)
                sb = rolled[key] if r == 0 else jnp.where(rowi >= r, rolled[key], sb)
            prev_key = key
        return fa, sb

    streams1 = [(p, lt) for lt in range(nlt) for p in range(2)]

    def head(hd, carry):
        ss = [st_s[2 * hd + p, lt] for p, lt in streams1]
        for r in range(PK_TOPK):
            for n, (p, lt) in enumerate(streams1):
                s = ss[n]
                m = jnp.max(s, axis=0, keepdims=True)
                ix = jnp.min(jnp.where(s == m, kiota, PK_NKEYS), axis=0, keepdims=True)
                tv_s[2 * hd + p, lt, r:r + 1, :] = m
                ti_s[2 * hd + p, lt, r:r + 1, :] = ix
                ss[n] = jnp.where(kiota == ix, -jnp.inf, s)
        cands, cidxs, bests = [], [], [[] for _ in range(nlt)]
        for lt in range(nlt):
            s1 = tv_s[2 * hd, lt]
            i1 = ti_s[2 * hd, lt]
            s2 = tv_s[2 * hd + 1, lt]
            i2 = ti_s[2 * hd + 1, lt]
            sums, ids = [], []
            for tile in tiles:
                va, vb = pair_rows(tile, s1, s2)
                ia, ib = pair_rows(tile, i1, i2)
                sums.append(va + vb)
                ids.append(ia * PK_NKEYS + ib)
            cands.append(jnp.concatenate(sums, axis=0))
            cidxs.append(jnp.concatenate(ids, axis=0))
        for r in range(PK_TOPK):
            for lt in range(nlt):
                cand = cands[lt]
                m = jnp.max(cand, axis=0, keepdims=True)
                pick = jnp.min(jnp.where(cand == m, pos, big), axis=0, keepdims=True)
                sel = pos == pick
                es_s[lt, hd, r:r + 1, :] = jnp.max(jnp.where(sel, cidxs[lt], -1), axis=0,
                                                   keepdims=True).astype(F32)
                bests[lt].append(m)
                cands[lt] = jnp.where(sel, -jnp.inf, cand)
        for lt in range(nlt):
            ex = [jnp.exp(b - bests[lt][0]) for b in bests[lt]]
            den = ex[0]
            for r in range(1, PK_TOPK):
                den = den + ex[r]
            for r in range(PK_TOPK):
                gs_s[lt, hd, r:r + 1, :] = ex[r] / den
        return carry

    lax.fori_loop(0, PK_HEADS, head, 0)
    for lt in range(nlt):
        rows = slice(lt * LANE, (lt + 1) * LANE)
        eidx_ref[:, rows] = es_s[lt].reshape(PK_PAIRS, LANE).astype(jnp.int32)
        gate_ref[rows, :] = gs_s[lt].reshape(PK_PAIRS, LANE).T


def _peer_select(xf, nf, wq, keys):
    n_tok = xf.shape[0]
    nlt = SEL_TB // LANE
    whole = lambda a: pl.BlockSpec(a.shape, lambda i: (0,) * a.ndim)
    return pl.pallas_call(
        _peer_select_kernel,
        grid=(n_tok // SEL_TB,),
        in_specs=[pl.BlockSpec((SEL_TB, D_MODEL), lambda i: (i, 0)), whole(nf), whole(wq), whole(keys)],
        out_specs=[pl.BlockSpec((PK_PAIRS, SEL_TB), lambda i: (0, i)),
                   pl.BlockSpec((SEL_TB, PK_PAIRS), lambda i: (i, 0))],
        out_shape=[jax.ShapeDtypeStruct((PK_PAIRS, n_tok), jnp.int32),
                   jax.ShapeDtypeStruct((n_tok, PK_PAIRS), F32)],
        scratch_shapes=[pltpu.VMEM((2 * PK_HEADS, nlt, PK_NKEYS, LANE), F32),
                        pltpu.VMEM((2 * PK_HEADS, nlt, PK_TOPK, LANE), F32),
                        pltpu.VMEM((2 * PK_HEADS, nlt, PK_TOPK, LANE), jnp.int32),
                        pltpu.VMEM((nlt, PK_HEADS, PK_TOPK, LANE), F32),
                        pltpu.VMEM((nlt, PK_HEADS, PK_TOPK, LANE), F32)],
        compiler_params=pltpu.CompilerParams(
            dimension_semantics=("arbitrary",), vmem_limit_bytes=VMEM_LIMIT),
        name="peer_select",
    )(xf, nf, wq, keys)


def _pack_kernel(t_ref, o_ref):
    bits = lax.bitcast_convert_type(t_ref[0].astype(BF16).astype(F32), jnp.uint32)
    word = (bits[:, :HALF] >> 16) | bits[:, HALF:]
    for s in range(HSUB):
        o_ref[:, s, :] = word[:, s * LANE:(s + 1) * LANE]


def _pack_table(tabs, layer):
    return pl.pallas_call(
        _pack_kernel,
        grid=(PK_EXPERTS // PACK_ROWS,),
        in_specs=[pl.BlockSpec((1, PACK_ROWS, D_MODEL), lambda i: (layer, i, 0))],
        out_specs=pl.BlockSpec((PACK_ROWS, HSUB, LANE), lambda i: (i, 0, 0)),
        out_shape=jax.ShapeDtypeStruct((PK_EXPERTS, HSUB, LANE), jnp.uint32),
        compiler_params=pltpu.CompilerParams(dimension_semantics=("arbitrary",)),
        name="pack_table",
    )(tabs)


N_PIECE = 2


def _split(x):
    p1 = x.astype(BF16).astype(F32)
    p2 = (x - p1).astype(BF16).astype(F32)
    return p1, p2


def _index_copies(idx_hbm, idx_smem, sems):
    first = pl.program_id(0) * PEER_TB
    return [pltpu.make_async_copy(idx_hbm.at[k, pl.ds(first, PEER_TB)], idx_smem[k], sems.at[k])
            for k in range(PK_PAIRS)]


def _gather_rows(tab_ref, idx_smem, t, stage):
    for k in range(PK_PAIRS):
        stage[pl.ds(k, HSUB, stride=STAGE_STRIDE), :] = tab_ref[idx_smem[k][t]]


def _stage_block(stage, s):
    return pltpu.bitcast(stage[s * STAGE_STRIDE:s * STAGE_STRIDE + PK_PAIRS, :], BF16)


PIPE_TOK = 2


def _token_pipeline(tab_ref, idx_smem, stages, compute):
    sets = (stages[:PIPE_TOK], stages[PIPE_TOK:])
    for j in range(PIPE_TOK):
        _gather_rows(tab_ref, idx_smem, j, sets[0][j])

    def step(t, cur, nxt):
        for j in range(PIPE_TOK):
            _gather_rows(tab_ref, idx_smem, jnp.minimum(t + PIPE_TOK + j, PEER_TB - 1), nxt[j])
        for j in range(PIPE_TOK):
            compute(t + j, cur[j])

    def body(i, carry):
        @pl.when(i % 2 == 0)
        def _():
            step(PIPE_TOK * i, sets[0], sets[1])

        @pl.when(i % 2 == 1)
        def _():
            step(PIPE_TOK * i, sets[1], sets[0])

        return carry

    lax.fori_loop(0, PEER_TB // PIPE_TOK, body, 0)


def _peer_u_kernel(idx_hbm, x_ref, nf_ref, gate_ref, tab_ref, w_ref, sems, hs, ybuf, *bufs):
    stages, idx_smem = bufs[:2 * PIPE_TOK], bufs[2 * PIPE_TOK:]
    copies = _index_copies(idx_hbm, idx_smem, sems)
    for cp in copies:
        cp.start()
    pieces = _split(_rms(x_ref[...], nf_ref[...]))
    hs[:, :, 2 * N_PIECE:, :] = jnp.zeros((PEER_TB, HSUB, SUB - 2 * N_PIECE, LANE), F32)
    for i in range(N_PIECE):
        for s in range(HSUB):
            hs[:, s, i, :] = pieces[i][:, s * LANE:(s + 1) * LANE]
            hs[:, s, N_PIECE + i, :] = pieces[i][:, HALF + s * LANE:HALF + (s + 1) * LANE]
    for cp in copies:
        cp.wait()
    rowi = lax.broadcasted_iota(jnp.int32, (SUB, 2 * PK_PAIRS), 0)
    odd = lax.broadcasted_iota(jnp.int32, (SUB, 2 * PK_PAIRS), 1) % 2
    keep = jnp.where(odd == 0, (rowi < N_PIECE).astype(F32),
                     ((rowi >= N_PIECE) & (rowi < 2 * N_PIECE)).astype(F32))

    def compute(t, stage):
        acc = jnp.zeros((SUB, 2 * PK_PAIRS), F32)
        for s in range(HSUB):
            acc = acc + lax.dot_general(hs[t, s].astype(BF16), _stage_block(stage, s), (((1,), (1,)), ((), ())),
                                        preferred_element_type=F32)
        ybuf[pl.ds(t, 1), :] = jnp.sum(acc * keep, axis=0, keepdims=True)

    _token_pipeline(tab_ref, idx_smem, stages, compute)
    r = lax.broadcasted_iota(jnp.int32, (2 * PK_PAIRS, PK_PAIRS), 0)
    c = lax.broadcasted_iota(jnp.int32, (2 * PK_PAIRS, PK_PAIRS), 1)
    fold = (r // 2 == c).astype(F32)
    act = jnp.dot(ybuf[...], fold, precision=HIGHEST, preferred_element_type=F32)
    w_ref[...] = gate_ref[...] * _gelu_tanh(act)


def _peer_u(eidx, xf, nf, gate, tab_packed):
    n_tok = xf.shape[0]
    stages = [pltpu.VMEM((HSUB * STAGE_STRIDE, LANE), jnp.uint32)] * (2 * PIPE_TOK)
    return pl.pallas_call(
        _peer_u_kernel,
        grid=(n_tok // PEER_TB,),
        in_specs=[
            pl.BlockSpec(memory_space=pl.ANY),
            pl.BlockSpec((PEER_TB, D_MODEL), lambda i: (i, 0)),
            pl.BlockSpec(nf.shape, lambda i: (0, 0)),
            pl.BlockSpec((PEER_TB, PK_PAIRS), lambda i: (i, 0)),
            pl.BlockSpec(memory_space=pltpu.VMEM),
        ],
        out_specs=pl.BlockSpec((PEER_TB, PK_PAIRS), lambda i: (i, 0)),
        out_shape=jax.ShapeDtypeStruct((n_tok, PK_PAIRS), F32),
        scratch_shapes=[
            pltpu.SemaphoreType.DMA((PK_PAIRS,)),
            pltpu.VMEM((PEER_TB, HSUB, SUB, LANE), F32),
            pltpu.VMEM((PEER_TB, 2 * PK_PAIRS), F32),
        ] + stages + [pltpu.SMEM((PEER_TB,), jnp.int32)] * PK_PAIRS,
        compiler_params=pltpu.CompilerParams(
            dimension_semantics=("arbitrary",), vmem_limit_bytes=VMEM_LIMIT),
        name="peer_u",
    )(eidx, xf, nf, gate, tab_packed)


def _peer_v_kernel(idx_hbm, w_ref, x_ref, tab_ref, o_ref, sems, av, ov, *bufs):
    stages, idx_smem = bufs[:2 * PIPE_TOK], bufs[2 * PIPE_TOK:]
    copies = _index_copies(idx_hbm, idx_smem, sems)
    for cp in copies:
        cp.start()
    r = lax.broadcasted_iota(jnp.int32, (PK_PAIRS, 2 * PK_PAIRS), 0)
    c = lax.broadcasted_iota(jnp.int32, (PK_PAIRS, 2 * PK_PAIRS), 1)
    spread = [(c == 2 * r).astype(BF16), (c == 2 * r + 1).astype(BF16)]
    pieces = _split(w_ref[...])
    av[:, 2 * N_PIECE:, :] = jnp.zeros((PEER_TB, SUB - 2 * N_PIECE, 2 * PK_PAIRS), F32)
    for i in range(N_PIECE):
        for j in range(2):
            av[:, N_PIECE * j + i, :] = jnp.dot(pieces[i].astype(BF16), spread[j], preferred_element_type=F32)
    for cp in copies:
        cp.wait()

    def compute(t, stage):
        wt = av[t].astype(BF16)
        lo, hi = [], []
        for s in range(HSUB):
            y = jnp.dot(wt, _stage_block(stage, s), preferred_element_type=F32)
            lo.append(y[0:1] + y[1:2])
            hi.append(y[2:3] + y[3:4])
        ov[pl.ds(t, 1), :] = jnp.concatenate(lo + hi, axis=1)

    _token_pipeline(tab_ref, idx_smem, stages, compute)
    o_ref[...] = x_ref[...] + ov[...]


def _peer_v(eidx, w, xf, tab_packed):
    n_tok = xf.shape[0]
    stages = [pltpu.VMEM((HSUB * STAGE_STRIDE, LANE), jnp.uint32)] * (2 * PIPE_TOK)
    return pl.pallas_call(
        _peer_v_kernel,
        grid=(n_tok // PEER_TB,),
        in_specs=[
            pl.BlockSpec(memory_space=pl.ANY),
            pl.BlockSpec((PEER_TB, PK_PAIRS), lambda i: (i, 0)),
            pl.BlockSpec((PEER_TB, D_MODEL), lambda i: (i, 0)),
            pl.BlockSpec(memory_space=pltpu.VMEM),
        ],
        out_specs=pl.BlockSpec((PEER_TB, D_MODEL), lambda i: (i, 0)),
        out_shape=jax.ShapeDtypeStruct((n_tok, D_MODEL), F32),
        scratch_shapes=[
            pltpu.SemaphoreType.DMA((PK_PAIRS,)),
            pltpu.VMEM((PEER_TB, SUB, 2 * PK_PAIRS), F32),
            pltpu.VMEM((PEER_TB, D_MODEL), F32),
        ] + stages + [pltpu.SMEM((PEER_TB,), jnp.int32)] * PK_PAIRS,
        compiler_params=pltpu.CompilerParams(
            dimension_semantics=("arbitrary",), vmem_limit_bytes=VMEM_LIMIT),
        name="peer_v",
    )(eidx, w, xf, tab_packed)


def _final_kernel(x_ref, g_ref, o_ref):
    o_ref[...] = _rms(x_ref[...], g_ref[...])


def _final_call(xf, g, *, n_rows, blk0):
    nblk = n_rows // MIX_TB
    return pl.pallas_call(
        _final_kernel,
        grid=(nblk,),
        in_specs=[pl.BlockSpec((MIX_TB, D_MODEL), lambda i: (blk0 + i, 0)), pl.BlockSpec(g.shape, lambda i: (0, 0))],
        out_specs=pl.BlockSpec((MIX_TB, D_MODEL), lambda i: (i, 0)),
        out_shape=jax.ShapeDtypeStruct((n_rows, D_MODEL), F32),
        compiler_params=pltpu.CompilerParams(dimension_semantics=("arbitrary",)),
        name="final_norm",
    )(xf, g)


def kernel(x_prompt, x_sample, state_conv, state_hgrn, norm_mix, norm_ffn, norm_final, ev_w_in, ev_a_ln_g, ev_a_ln_b, ev_ws, ev_bs, ev_conv_w, ev_conv_b, ev_b_ln_g, ev_b_ln_b, ev_w_out, od_w_in, od_lower, od_norm_g, od_w_out, peer_wq, peer_keys, peer_u, peer_v):
    row = lambda a: a.reshape(1, -1)
    xf = jnp.concatenate([x_prompt.reshape(N_PROMPT, D_MODEL), x_sample.reshape(N_SAMPLE, D_MODEL)], axis=0)
    pad_hist = lambda a: jnp.pad(a, ((0, 0), (CV_PAD - CV_STATE, 0), (0, 0)))
    conv_p, conv_s, hg_p, hg_s, v_s = [], [], [], [], []
    prompt = dict(nb=BATCH, t_len=SEQ, tb=MIX_TB, blk0=0)
    sample = dict(nb=DEC_BATCH, t_len=DEC_SEQ, tb=DEC_SEQ, blk0=N_PROMPT // DEC_SEQ)
    for l in range(DEPTH):
        j = l // 2
        if l % 2 == 0:
            prm = (row(norm_mix[l]), ev_w_in[j].astype(BF16), row(ev_a_ln_g[j]), row(ev_a_ln_b[j]), ev_ws[j],
                   ev_bs[j].T, ev_conv_w[j], row(ev_conv_b[j]), row(ev_b_ln_g[j]), row(ev_b_ln_b[j]),
                   ev_w_out[j].astype(BF16))
            zeros = jnp.zeros((BATCH, CV_PAD, CV_WIDTH), F32)
            xf, cp = _even_call(xf, zeros, *prm, emit_v=False, **prompt)
            xf, cs, vs = _even_call(xf, pad_hist(state_conv[j]), *prm, emit_v=True, **sample)
            conv_p.append(cp[:, CV_PAD - CV_STATE:])
            conv_s.append(cs[:, CV_PAD - CV_STATE:])
            v_s.append(vs)
        else:
            prm = (row(norm_mix[l]), od_lower, od_w_in[j].astype(BF16), row(od_norm_g[j]), od_w_out[j].astype(BF16))
            zeros = jnp.zeros((BATCH, HG_HEADS, HG_DK, HG_DK), F32)
            xf, sp = _odd_call(xf, zeros, *prm, layer=l, **prompt)
            xf, ss = _odd_call(xf, state_hgrn[j], *prm, layer=l, **sample)
            hg_p.append(sp)
            hg_s.append(ss)
        nf = row(norm_ffn[l])
        keys = peer_keys[l].reshape(2 * PK_HEADS, PK_NKEYS, PK_HALF).astype(BF16)
        eidx, gate = _peer_select(xf, nf, peer_wq[l].astype(BF16), keys)
        w = _peer_u(eidx, xf, nf, gate, _pack_table(peer_u, l))
        xf = _peer_v(eidx, w, xf, _pack_table(peer_v, l))
    g = row(norm_final)
    y_prompt = _final_call(xf, g, n_rows=N_PROMPT, blk0=0).reshape(BATCH, SEQ, D_MODEL)
    y_sample = _final_call(xf, g, n_rows=N_SAMPLE, blk0=N_PROMPT // MIX_TB).reshape(DEC_BATCH, DEC_SEQ, D_MODEL)
    return (y_prompt, y_sample, jnp.stack(conv_p), jnp.stack(conv_s), jnp.stack(hg_p), jnp.stack(hg_s),
            jnp.stack(v_s))
```
